```python
import math
import jax, jax.numpy as jnp
from jax import lax
import numpy as np

D_MODEL = 2048
BATCH = 4
SEQ = 4096
DEPTH = 4

HEAD_DIM = 128
V_HEAD_DIM = 2 * HEAD_DIM
N_DIFF_HEADS = D_MODEL // V_HEAD_DIM
ATTN_WIDTH = N_DIFF_HEADS * V_HEAD_DIM
Q_BLOCK = 128
ROPE_THETA = 10000.0
LAMBDA_INIT_SCALE = 0.1
CHUNK = 128
N_SGU_GROUPS = 8
SGU_WIDTH = D_MODEL
SGU_GROUP_DIM = SGU_WIDTH // N_SGU_GROUPS
N_BRANCHES = 2
FFN_HIDDEN = -(-8 * D_MODEL // (3 * 256)) * 256
EPS = 1e-6

Q_W = N_DIFF_HEADS * 2 * HEAD_DIM
K_W = N_DIFF_HEADS * 2 * HEAD_DIM
V_W = ATTN_WIDTH
Z_W = 2 * SGU_WIDTH
GATE_W = N_BRANCHES * D_MODEL
IN_COLS = Q_W + K_W + V_W + Z_W + GATE_W

kernel_name = 'hybrid_diffattn_sgu_gated_block'


def rms_norm(x, g):
    x32 = x.astype(jnp.float32)
    y = x32 * lax.rsqrt(jnp.mean(x32 * x32, axis=-1, keepdims=True) + EPS)
    return (y * g.astype(jnp.float32)).astype(x.dtype)


def layer_norm(x, g, b):
    x32 = x.astype(jnp.float32)
    mu = jnp.mean(x32, axis=-1, keepdims=True)
    var = jnp.mean(jnp.square(x32 - mu), axis=-1, keepdims=True)
    y = (x32 - mu) * lax.rsqrt(var + EPS)
    return (y * g.astype(jnp.float32) + b.astype(jnp.float32)).astype(x.dtype)


def rope_tables(positions, dtype):
    inv_freq = ROPE_THETA ** (-jnp.arange(0, HEAD_DIM, 2, dtype=jnp.float32) / HEAD_DIM)
    ang = positions.astype(jnp.float32)[..., None] * inv_freq
    ang = jnp.concatenate([ang, ang], axis=-1)[:, :, None, None, :]
    return jnp.cos(ang).astype(dtype), jnp.sin(ang).astype(dtype)


def apply_rope(t, cos, sin):
    t1, t2 = jnp.split(t, 2, axis=-1)
    return t * cos + jnp.concatenate([-t2, t1], axis=-1) * sin


def diff_attention(q, k, v, lam):
    S = q.shape[1]
    scale = HEAD_DIM ** -0.5
    outs = []
    for i in range(S // Q_BLOCK):
        q0 = i * Q_BLOCK
        kv_len = q0 + Q_BLOCK
        qb = q[:, q0:kv_len]
        kb = k[:, :kv_len]
        vb = v[:, :kv_len]
        s = jnp.einsum('bqhcd,bkhcd->bhcqk', qb, kb).astype(jnp.float32) * scale
        causal = (q0 + jnp.arange(Q_BLOCK))[:, None] >= jnp.arange(kv_len)[None, :]
        s = jnp.where(causal, s, -jnp.inf)
        p = jax.nn.softmax(s, axis=-1)
        a = p[:, :, 0] - lam * p[:, :, 1]
        outs.append(jnp.einsum('bhqk,bkhe->bqhe', a.astype(vb.dtype), vb))
    return jnp.concatenate(outs, axis=1)


def chunk_spatial_gating(u, vg, ln_g, ln_b, w_s, b_s):
    B, S, _ = u.shape
    vg = layer_norm(vg, ln_g, ln_b)
    vg = vg.reshape(B, S // CHUNK, CHUNK, N_SGU_GROUPS, SGU_GROUP_DIM)
    w = w_s * jnp.tril(jnp.ones((CHUNK, CHUNK), dtype=w_s.dtype))
    mixed = jnp.einsum('gij,bnjgc->bnigc', w, vg) + b_s.T[:, :, None]
    return u * mixed.reshape(B, S, SGU_WIDTH)


def setup_inputs(seed: int = 0) -> dict:
    key = jax.random.key(seed)
    ks = jax.random.split(key, 18)
    f32 = jnp.float32
    x = jax.random.normal(ks[0], (BATCH, SEQ, D_MODEL), f32)
    offsets = jax.random.randint(ks[1], (BATCH, 1), 0, 1024, dtype=jnp.int32)
    positions = offsets + jnp.arange(SEQ, dtype=jnp.int32)[None, :]
    norm_attn = 1.0 + 0.1 * jax.random.normal(ks[2], (DEPTH, D_MODEL), f32)
    w_in = jax.random.normal(ks[3], (DEPTH, D_MODEL, IN_COLS), f32) * D_MODEL ** -0.5
    gate_b = 0.1 * jax.random.normal(ks[4], (DEPTH, GATE_W), f32)
    lam = LAMBDA_INIT_SCALE * jax.random.normal(ks[5], (DEPTH, 4, HEAD_DIM), f32)
    subln = 1.0 + 0.1 * jax.random.normal(ks[6], (DEPTH, V_HEAD_DIM), f32)
    sgu_ln_g = 1.0 + 0.1 * jax.random.normal(ks[7], (DEPTH, SGU_WIDTH), f32)
    sgu_ln_b = 0.02 * jax.random.normal(ks[8], (DEPTH, SGU_WIDTH), f32)
    w_spatial = jax.random.normal(ks[9], (DEPTH, N_SGU_GROUPS, CHUNK, CHUNK), f32) * CHUNK ** -0.5
    b_spatial = 1.0 + 0.1 * jax.random.normal(ks[10], (DEPTH, N_SGU_GROUPS, CHUNK), f32)
    w_branch = jax.random.normal(ks[11], (DEPTH, N_BRANCHES, ATTN_WIDTH, D_MODEL), f32) * ATTN_WIDTH ** -0.5
    w_out = jax.random.normal(ks[12], (DEPTH, D_MODEL, D_MODEL), f32) * D_MODEL ** -0.5
    norm_ffn = 1.0 + 0.1 * jax.random.normal(ks[13], (DEPTH, D_MODEL), f32)
    w_ffn_in = jax.random.normal(ks[14], (DEPTH, D_MODEL, 2 * FFN_HIDDEN), f32) * D_MODEL ** -0.5
    w_ffn_out = jax.random.normal(ks[15], (DEPTH, FFN_HIDDEN, D_MODEL), f32) * FFN_HIDDEN ** -0.5
    norm_final = 1.0 + 0.1 * jax.random.normal(ks[16], (D_MODEL,), f32)
    return {'x': x, 'positions': positions, 'norm_attn': norm_attn, 'w_in': w_in,
            'gate_b': gate_b, 'lam': lam, 'subln': subln, 'sgu_ln_g': sgu_ln_g,
            'sgu_ln_b': sgu_ln_b, 'w_spatial': w_spatial, 'b_spatial': b_spatial,
            'w_branch': w_branch, 'w_out': w_out, 'norm_ffn': norm_ffn,
            'w_ffn_in': w_ffn_in, 'w_ffn_out': w_ffn_out, 'norm_final': norm_final}


def reference(x, positions, norm_attn, w_in, gate_b, lam, subln, sgu_ln_g, sgu_ln_b,
              w_spatial, b_spatial, w_branch, w_out, norm_ffn, w_ffn_in, w_ffn_out,
              norm_final):
    B, S, _ = x.shape
    cos, sin = rope_tables(positions, x.dtype)
    splits = [Q_W, Q_W + K_W, Q_W + K_W + V_W, Q_W + K_W + V_W + Z_W]
    for l in range(DEPTH):
        h = rms_norm(x, norm_attn[l])
        proj = h @ w_in[l]
        q, k, v, z, gate_logits = jnp.split(proj, splits, axis=-1)

        q = apply_rope(q.reshape(B, S, N_DIFF_HEADS, 2, HEAD_DIM), cos, sin)
        k = apply_rope(k.reshape(B, S, N_DIFF_HEADS, 2, HEAD_DIM), cos, sin)
        v = v.reshape(B, S, N_DIFF_HEADS, V_HEAD_DIM)
        lam_init = 0.8 - 0.6 * math.exp(-0.3 * l)
        lp = lam[l].astype(jnp.float32)
        lam_full = jnp.exp(jnp.sum(lp[0] * lp[1])) - jnp.exp(jnp.sum(lp[2] * lp[3])) + lam_init
        o = diff_attention(q, k, v, lam_full)
        o = rms_norm(o, subln[l]) * (1.0 - lam_init)
        y_attn = o.reshape(B, S, ATTN_WIDTH) @ w_branch[l, 0]

        u, vg = jnp.split(jax.nn.gelu(z, approximate=False), 2, axis=-1)
        sg = chunk_spatial_gating(u, vg, sgu_ln_g[l], sgu_ln_b[l], w_spatial[l], b_spatial[l])
        y_sgu = sg @ w_branch[l, 1]

        g_attn, g_sgu = jnp.split(jax.nn.sigmoid(gate_logits + gate_b[l]), 2, axis=-1)
        x = x + (g_attn * y_attn + g_sgu * y_sgu) @ w_out[l]

        h = rms_norm(x, norm_ffn[l])
        a, b = jnp.split(h @ w_ffn_in[l], 2, axis=-1)
        x = x + (jax.nn.silu(a) * b) @ w_ffn_out[l]
    return rms_norm(x, norm_final)
```

```python
import functools
import math

import jax
import jax.numpy as jnp
from jax import lax
from jax.experimental import pallas as pl
from jax.experimental.pallas import tpu as pltpu

ROPE_THETA = 10000.0
EPS = 1e-6
LANES = 128
VMEM_LIMIT_BYTES = 56 * 1024 * 1024

F32 = jnp.float32
BF16 = jnp.bfloat16


def _blk(dim, pref):
    if dim <= pref:
        return dim
    b = pref - pref % LANES
    while b > LANES and dim % b:
        b -= LANES
    assert dim % b == 0, (dim, pref)
    return b


def _params(*sem):
    return pltpu.CompilerParams(dimension_semantics=sem, vmem_limit_bytes=VMEM_LIMIT_BYTES)


def _mm(a, b):
    return jnp.dot(a, b, preferred_element_type=F32)


def _rope_kernel(pos_ref, invf_ref, cos_ref, sin_ref):
    ang = pos_ref[...].astype(F32) * invf_ref[...]
    lane = lax.broadcasted_iota(jnp.int32, ang.shape, 1)
    sin = jnp.sin(ang)
    cos_ref[...] = jnp.cos(ang)
    sin_ref[...] = jnp.where(lane < ang.shape[1] // 2, -sin, sin)


def _rope_tables(pos, invf):
    T = pos.shape[0]
    hd = invf.shape[1]
    tm = _blk(T, 2048)
    return pl.pallas_call(
        _rope_kernel,
        grid=(T // tm,),
        in_specs=[pl.BlockSpec((tm, 1), lambda i: (i, 0)),
                  pl.BlockSpec((1, hd), lambda i: (0, 0))],
        out_specs=[pl.BlockSpec((tm, hd), lambda i: (i, 0)),
                   pl.BlockSpec((tm, hd), lambda i: (i, 0))],
        out_shape=[jax.ShapeDtypeStruct((T, hd), F32)] * 2,
        compiler_params=_params("parallel"),
        name="rope_tables",
    )(pos, invf)


def _rmsnorm_kernel(x_ref, g_ref, o_ref):
    x = x_ref[...]
    y = x * lax.rsqrt(jnp.mean(x * x, axis=-1, keepdims=True) + EPS)
    o_ref[...] = (y * g_ref[...]).astype(o_ref.dtype)


def _rmsnorm(x, g, out_dtype):
    T, D = x.shape
    tm = _blk(T, 512)
    return pl.pallas_call(
        _rmsnorm_kernel,
        grid=(T // tm,),
        in_specs=[pl.BlockSpec((tm, D), lambda i: (i, 0)),
                  pl.BlockSpec((1, D), lambda i: (0, 0))],
        out_specs=pl.BlockSpec((tm, D), lambda i: (i, 0)),
        out_shape=jax.ShapeDtypeStruct((T, D), out_dtype),
        compiler_params=_params("parallel"),
        name="rmsnorm",
    )(x, g.reshape(1, D))


def _qkv_kernel(h_ref, w_ref, cos_ref, sin_ref, o_ref, *, n_rope_blocks, hd):
    j = pl.program_id(1)
    acc = _mm(h_ref[...], w_ref[...])

    @pl.when(j < n_rope_blocks)
    def _():
        cos = cos_ref[...]
        sin = sin_ref[...]
        for s in range(acc.shape[1] // hd):
            t = acc[:, s * hd:(s + 1) * hd]
            r = t * cos + pltpu.roll(t, hd // 2, 1) * sin
            o_ref[:, s * hd:(s + 1) * hd] = r.astype(o_ref.dtype)

    @pl.when(j >= n_rope_blocks)
    def _():
        o_ref[...] = acc.astype(o_ref.dtype)


def _qkv_proj(h, w, cos, sin, n_rope_cols):
    T, D = h.shape
    N = w.shape[1]
    hd = cos.shape[1]
    tm, tn = _blk(T, 1024), _blk(n_rope_cols, 1024)
    return pl.pallas_call(
        functools.partial(_qkv_kernel, n_rope_blocks=n_rope_cols // tn, hd=hd),
        grid=(T // tm, N // tn),
        in_specs=[pl.BlockSpec((tm, D), lambda i, j: (i, 0)),
                  pl.BlockSpec((D, tn), lambda i, j: (0, j)),
                  pl.BlockSpec((tm, hd), lambda i, j: (i, 0)),
                  pl.BlockSpec((tm, hd), lambda i, j: (i, 0))],
        out_specs=pl.BlockSpec((tm, tn), lambda i, j: (i, j)),
        out_shape=jax.ShapeDtypeStruct((T, N), BF16),
        compiler_params=_params("parallel", "arbitrary"),
        name="qkv_proj",
    )(h, w, cos, sin)


def _attn_kernel(scal_ref, lam_ref, subln_ref, q_ref, k_ref, v_ref, o_ref,
                 m1_ref, l1_ref, a1_ref, m2_ref, l2_ref, a2_ref, *, blk, hd):
    qi = pl.program_id(2)
    scale = hd ** -0.5
    q1 = q_ref[:, :hd]
    q2 = q_ref[:, hd:]

    for m_ref, l_ref, a_ref in ((m1_ref, l1_ref, a1_ref), (m2_ref, l2_ref, a2_ref)):
        m_ref[...] = jnp.full(m_ref.shape, -jnp.inf, F32)
        l_ref[...] = jnp.zeros(l_ref.shape, F32)
        a_ref[...] = jnp.zeros(a_ref.shape, F32)

    def update(s, v, m_ref, l_ref, a_ref):
        m_old = m_ref[...]
        m_new = jnp.maximum(m_old, jnp.max(s, axis=-1, keepdims=True))
        alpha = jnp.exp((m_old - m_new) * scale)
        p = jnp.exp((s - m_new) * scale)
        l_ref[...] = alpha * l_ref[...] + jnp.sum(p, axis=-1, keepdims=True)
        a_ref[...] = alpha * a_ref[...] + _mm(p.astype(v.dtype), v)
        m_ref[...] = m_new

    def scores(q, k):
        return lax.dot_general(q, k, (((1,), (1,)), ((), ())), preferred_element_type=F32)

    def step(j, masked):
        rows = pl.ds(pl.multiple_of(j * blk, blk), blk)
        k = k_ref[rows, :]
        v = v_ref[rows, :]
        s1 = scores(q1, k[:, :hd])
        s2 = scores(q2, k[:, hd:])
        if masked:
            r = lax.broadcasted_iota(jnp.int32, s1.shape, 0)
            c = lax.broadcasted_iota(jnp.int32, s1.shape, 1)
            s1 = jnp.where(r >= c, s1, -jnp.inf)
            s2 = jnp.where(r >= c, s2, -jnp.inf)
        update(s1, v, m1_ref, l1_ref, a1_ref)
        update(s2, v, m2_ref, l2_ref, a2_ref)

    def body(j, carry):
        step(j, masked=False)
        return carry

    lax.fori_loop(0, qi, body, 0)
    step(qi, masked=True)

    lp = lam_ref[...]
    lam_init = scal_ref[0]
    lam_full = (jnp.exp(jnp.sum(lp[0:1] * lp[1:2], keepdims=True))
                - jnp.exp(jnp.sum(lp[2:3] * lp[3:4], keepdims=True)) + lam_init)
    o = a1_ref[...] / l1_ref[...] - lam_full * (a2_ref[...] / l2_ref[...])
    y = o * lax.rsqrt(jnp.mean(o * o, axis=-1, keepdims=True) + EPS)
    o_ref[...] = ((y * subln_ref[...]) * (1.0 - lam_init)).astype(o_ref.dtype)


def _diff_attention(qkv, lam_l, subln_l, lam_init, B, S, H, hd, blk_pref=512):
    vhd = 2 * hd
    blk = _blk(S, blk_pref)
    qkv3 = qkv.reshape(B, S, 3 * H * vhd)
    scal = jnp.full((1,), lam_init, F32)
    return pl.pallas_call(
        functools.partial(_attn_kernel, blk=blk, hd=hd),
        grid=(B, H, S // blk),
        in_specs=[pl.BlockSpec(memory_space=pltpu.SMEM),
                  pl.BlockSpec((4, hd), lambda b, h, i: (0, 0)),
                  pl.BlockSpec((1, vhd), lambda b, h, i: (0, 0)),
                  pl.BlockSpec((None, blk, vhd), lambda b, h, i: (b, i, h)),
                  pl.BlockSpec((None, S, vhd), lambda b, h, i: (b, 0, H + h)),
                  pl.BlockSpec((None, S, vhd), lambda b, h, i: (b, 0, 2 * H + h))],
        out_specs=pl.BlockSpec((None, blk, vhd), lambda b, h, i: (b, i, h)),
        out_shape=jax.ShapeDtypeStruct((B, S, H * vhd), BF16),
        scratch_shapes=[pltpu.VMEM((blk, 1), F32), pltpu.VMEM((blk, 1), F32),
                        pltpu.VMEM((blk, vhd), F32),
                        pltpu.VMEM((blk, 1), F32), pltpu.VMEM((blk, 1), F32),
                        pltpu.VMEM((blk, vhd), F32)],
        compiler_params=_params("parallel", "parallel", "arbitrary"),
        name="diff_attention",
    )(scal, lam_l, subln_l.reshape(1, vhd), qkv3, qkv3, qkv3).reshape(B * S, H * vhd)


def _sgu_kernel(h_ref, wz_ref, lng_ref, lnb_ref, ws_ref, bs_ref, o_ref, *, chunk, groups):
    W = o_ref.shape[1]
    gd = W // groups
    z = _mm(h_ref[...], wz_ref[...])
    gz = 0.5 * z * (1.0 + lax.erf(z * (2.0 ** -0.5)))
    u = gz[:, :W]
    vg = gz[:, W:]
    mu = jnp.mean(vg, axis=-1, keepdims=True)
    d = vg - mu
    var = jnp.mean(d * d, axis=-1, keepdims=True)
    vn = ((d * lax.rsqrt(var + EPS)) * lng_ref[...] + lnb_ref[...]).astype(BF16)
    r = lax.broadcasted_iota(jnp.int32, (chunk, chunk), 0)
    c = lax.broadcasted_iota(jnp.int32, (chunk, chunk), 1)
    causal = r >= c
    for g in range(groups):
        w = jnp.where(causal, ws_ref[g], jnp.zeros((), ws_ref.dtype))
        bias = bs_ref[:, g:g + 1]
        cols = slice(g * gd, (g + 1) * gd)
        for n in range(o_ref.shape[0] // chunk):
            rows = slice(n * chunk, (n + 1) * chunk)
            mixed = _mm(w, vn[rows, cols]) + bias
            o_ref[rows, cols] = (u[rows, cols] * mixed).astype(o_ref.dtype)


def _sgu(h, wz, ln_g, ln_b, ws, bs_t, chunk, groups):
    T, D = h.shape
    W = wz.shape[1] // 2
    tm = 2 * chunk if T % (2 * chunk) == 0 else chunk
    return pl.pallas_call(
        functools.partial(_sgu_kernel, chunk=chunk, groups=groups),
        grid=(T // tm,),
        in_specs=[pl.BlockSpec((tm, D), lambda i: (i, 0)),
                  pl.BlockSpec((D, 2 * W), lambda i: (0, 0)),
                  pl.BlockSpec((1, W), lambda i: (0, 0)),
                  pl.BlockSpec((1, W), lambda i: (0, 0)),
                  pl.BlockSpec((groups, chunk, chunk), lambda i: (0, 0, 0)),
                  pl.BlockSpec((chunk, groups), lambda i: (0, 0))],
        out_specs=pl.BlockSpec((tm, W), lambda i: (i, 0)),
        out_shape=jax.ShapeDtypeStruct((T, W), BF16),
        compiler_params=_params("parallel"),
        name="sgu",
    )(h, wz, ln_g.reshape(1, W), ln_b.reshape(1, W), ws, bs_t)


def _merge_kernel(h_ref, o_ref, sg_ref, wga_ref, wgs_ref, wa_ref, ws_ref, ba_ref, bs_ref, out_ref):
    h = h_ref[...]
    g_attn = jax.nn.sigmoid(_mm(h, wga_ref[...]) + ba_ref[...])
    g_sgu = jax.nn.sigmoid(_mm(h, wgs_ref[...]) + bs_ref[...])
    y = g_attn * _mm(o_ref[...], wa_ref[...]) + g_sgu * _mm(sg_ref[...], ws_ref[...])
    out_ref[...] = y.astype(out_ref.dtype)


def _merge(h, o, sg, w_gate, gate_b, w_attn, w_sgu):
    T, D = h.shape
    tm, tn = _blk(T, 512), _blk(D, 512)
    nb = D // tn
    row = lambda j, i: (i, 0)
    return pl.pallas_call(
        _merge_kernel,
        grid=(nb, T // tm),
        in_specs=[pl.BlockSpec((tm, D), row),
                  pl.BlockSpec((tm, o.shape[1]), row),
                  pl.BlockSpec((tm, sg.shape[1]), row),
                  pl.BlockSpec((D, tn), lambda j, i: (0, j)),
                  pl.BlockSpec((D, tn), lambda j, i: (0, nb + j)),
                  pl.BlockSpec((w_attn.shape[0], tn), lambda j, i: (0, j)),
                  pl.BlockSpec((w_sgu.shape[0], tn), lambda j, i: (0, j)),
                  pl.BlockSpec((1, tn), lambda j, i: (0, j)),
                  pl.BlockSpec((1, tn), lambda j, i: (0, nb + j))],
        out_specs=pl.BlockSpec((tm, tn), lambda j, i: (i, j)),
        out_shape=jax.ShapeDtypeStruct((T, D), BF16),
        compiler_params=_params("parallel", "arbitrary"),
        name="merge",
    )(h, o, sg, w_gate, w_gate, w_attn, w_sgu, gate_b.reshape(1, -1), gate_b.reshape(1, -1))


def _proj_res_kernel(a_ref, w_ref, x_ref, o_ref):
    o_ref[...] = x_ref[...] + _mm(a_ref[...], w_ref[...])


def _proj_residual(a, w, x):
    T, K = a.shape
    N = w.shape[1]
    tm, tn = _blk(T, 512), _blk(N, 1024)
    return pl.pallas_call(
        _proj_res_kernel,
        grid=(N // tn, T // tm),
        in_specs=[pl.BlockSpec((tm, K), lambda j, i: (i, 0)),
                  pl.BlockSpec((K, tn), lambda j, i: (0, j)),
                  pl.BlockSpec((tm, tn), lambda j, i: (i, j))],
        out_specs=pl.BlockSpec((tm, tn), lambda j, i: (i, j)),
        out_shape=jax.ShapeDtypeStruct((T, N), F32),
        compiler_params=_params("parallel", "arbitrary"),
        name="proj_residual",
    )(a, w, x)


def _ffn_in_kernel(h_ref, wa_ref, wb_ref, o_ref):
    h = h_ref[...]
    a = _mm(h, wa_ref[...])
    b = _mm(h, wb_ref[...])
    o_ref[...] = (a * jax.nn.sigmoid(a) * b).astype(o_ref.dtype)


def _ffn_in(h, w):
    T, D = h.shape
    F = w.shape[1] // 2
    tm, tn = _blk(T, 1024), _blk(F, 512)
    nb = F // tn
    return pl.pallas_call(
        _ffn_in_kernel,
        grid=(nb, T // tm),
        in_specs=[pl.BlockSpec((tm, D), lambda j, i: (i, 0)),
                  pl.BlockSpec((D, tn), lambda j, i: (0, j)),
                  pl.BlockSpec((D, tn), lambda j, i: (0, nb + j))],
        out_specs=pl.BlockSpec((tm, tn), lambda j, i: (i, j)),
        out_shape=jax.ShapeDtypeStruct((T, F), BF16),
        compiler_params=_params("parallel", "arbitrary"),
        name="ffn_in",
    )(h, w, w)


def kernel(x, positions, norm_attn, w_in, gate_b, lam, subln, sgu_ln_g, sgu_ln_b, w_spatial,
           b_spatial, w_branch, w_out, norm_ffn, w_ffn_in, w_ffn_out, norm_final):
    B, S, D = x.shape
    T = B * S
    depth = w_in.shape[0]
    hd = lam.shape[-1]
    H = D // subln.shape[-1]
    groups, chunk = w_spatial.shape[1], w_spatial.shape[2]
    W = sgu_ln_g.shape[-1]
    assert S % chunk == 0 and w_in.shape[2] == 3 * D + 2 * W + 2 * D

    inv_freq = ROPE_THETA ** (-jnp.arange(0, hd, 2, dtype=F32) / hd)
    invf = jnp.concatenate([inv_freq, inv_freq]).reshape(1, hd)
    cos, sin = _rope_tables(positions.reshape(T, 1), invf)

    xf = x.reshape(T, D)
    for l in range(depth):
        lam_init = 0.8 - 0.6 * math.exp(-0.3 * l)
        w_qkv = w_in[l, :, :3 * D].astype(BF16)
        w_z = w_in[l, :, 3 * D:3 * D + 2 * W].astype(BF16)
        w_gate = w_in[l, :, 3 * D + 2 * W:].astype(BF16)

        h = _rmsnorm(xf, norm_attn[l], BF16)
        qkv = _qkv_proj(h, w_qkv, cos, sin, 2 * D)
        o = _diff_attention(qkv, lam[l], subln[l], lam_init, B, S, H, hd)
        sg = _sgu(h, w_z, sgu_ln_g[l], sgu_ln_b[l], w_spatial[l].astype(BF16),
                  b_spatial[l].T, chunk, groups)
        merged = _merge(h, o, sg, w_gate, gate_b[l],
                        w_branch[l, 0].astype(BF16), w_branch[l, 1].astype(BF16))
        xf = _proj_residual(merged, w_out[l].astype(BF16), xf)

        h2 = _rmsnorm(xf, norm_ffn[l], BF16)
        act = _ffn_in(h2, w_ffn_in[l].astype(BF16))
        xf = _proj_residual(act, w_ffn_out[l].astype(BF16), xf)

    return _rmsnorm(xf, norm_final, x.dtype).reshape(B, S, D)
```

```python
import functools
import math

import jax
import jax.numpy as jnp
from jax import lax
from jax.experimental import pallas as pl
from jax.experimental.pallas import tpu as pltpu

ROPE_THETA = 10000.0
EPS = 1e-6
LANES = 128
VMEM_LIMIT_BYTES = 56 * 1024 * 1024

F32 = jnp.float32
BF16 = jnp.bfloat16


def _blk(dim, pref):
    if dim <= pref:
        return dim
    b = pref - pref % LANES
    while b > LANES and dim % b:
        b -= LANES
    assert dim % b == 0, (dim, pref)
    return b


def _params(*sem):
    return pltpu.CompilerParams(dimension_semantics=sem, vmem_limit_bytes=VMEM_LIMIT_BYTES)


def _mm(a, b):
    return jnp.dot(a, b, preferred_element_type=F32)


def _rope_kernel(pos_ref, invf_ref, cos_ref, sin_ref):
    ang = pos_ref[...].astype(F32) * invf_ref[...]
    lane = lax.broadcasted_iota(jnp.int32, ang.shape, 1)
    sin = jnp.sin(ang)
    cos_ref[...] = jnp.cos(ang)
    sin_ref[...] = jnp.where(lane < ang.shape[1] // 2, -sin, sin)


def _rope_tables(pos, invf):
    T = pos.shape[0]
    hd = invf.shape[1]
    tm = _blk(T, 2048)
    return pl.pallas_call(
        _rope_kernel,
        grid=(T // tm,),
        in_specs=[pl.BlockSpec((tm, 1), lambda i: (i, 0)),
                  pl.BlockSpec((1, hd), lambda i: (0, 0))],
        out_specs=[pl.BlockSpec((tm, hd), lambda i: (i, 0)),
                   pl.BlockSpec((tm, hd), lambda i: (i, 0))],
        out_shape=[jax.ShapeDtypeStruct((T, hd), F32)] * 2,
        compiler_params=_params("parallel"),
        name="rope_tables",
    )(pos, invf)


def _rmsnorm_kernel(x_ref, g_ref, o_ref):
    x = x_ref[...]
    y = x * lax.rsqrt(jnp.mean(x * x, axis=-1, keepdims=True) + EPS)
    o_ref[...] = (y * g_ref[...]).astype(o_ref.dtype)


def _rmsnorm(x, g, out_dtype):
    T, D = x.shape
    tm = _blk(T, 512)
    return pl.pallas_call(
        _rmsnorm_kernel,
        grid=(T // tm,),
        in_specs=[pl.BlockSpec((tm, D), lambda i: (i, 0)),
                  pl.BlockSpec((1, D), lambda i: (0, 0))],
        out_specs=pl.BlockSpec((tm, D), lambda i: (i, 0)),
        out_shape=jax.ShapeDtypeStruct((T, D), out_dtype),
        compiler_params=_params("parallel"),
        name="rmsnorm",
    )(x, g.reshape(1, D))


def _qkv_kernel(h_ref, w_ref, cos_ref, sin_ref, o_ref, *, n_rope_blocks, hd):
    j = pl.program_id(1)
    acc = _mm(h_ref[...], w_ref[...])

    @pl.when(j < n_rope_blocks)
    def _():
        cos = cos_ref[...]
        sin = sin_ref[...]
        for s in range(acc.shape[1] // hd):
            t = acc[:, s * hd:(s + 1) * hd]
            r = t * cos + pltpu.roll(t, hd // 2, 1) * sin
            o_ref[:, s * hd:(s + 1) * hd] = r.astype(o_ref.dtype)

    @pl.when(j >= n_rope_blocks)
    def _():
        o_ref[...] = acc.astype(o_ref.dtype)


def _qkv_proj(h, w, cos, sin, n_rope_cols):
    T, D = h.shape
    N = w.shape[1]
    hd = cos.shape[1]
    tm, tn = _blk(T, 1024), _blk(n_rope_cols, 1024)
    return pl.pallas_call(
        functools.partial(_qkv_kernel, n_rope_blocks=n_rope_cols // tn, hd=hd),
        grid=(T // tm, N // tn),
        in_specs=[pl.BlockSpec((tm, D), lambda i, j: (i, 0)),
                  pl.BlockSpec((D, tn), lambda i, j: (0, j)),
                  pl.BlockSpec((tm, hd), lambda i, j: (i, 0)),
                  pl.BlockSpec((tm, hd), lambda i, j: (i, 0))],
        out_specs=pl.BlockSpec((tm, tn), lambda i, j: (i, j)),
        out_shape=jax.ShapeDtypeStruct((T, N), BF16),
        compiler_params=_params("parallel", "arbitrary"),
        name="qkv_proj",
    )(h, w, cos, sin)


def _attn_kernel(scal_ref, lam_ref, subln_ref, q_ref, k_ref, v_ref, o_ref,
                 vt_ref, sa_ref, sb_ref, m1_ref, l1_ref, a1_ref, m2_ref, l2_ref, a2_ref,
                 *, blk, hd):
    qi = pl.program_id(2)
    c = hd ** -0.5 * math.log2(math.e)

    @pl.when(qi == 0)
    def _():
        for n in range(vt_ref.shape[0]):
            vt_ref[n] = v_ref[n * blk:(n + 1) * blk, :].T

    qt1 = q_ref[:, :hd].T
    qt2 = q_ref[:, hd:].T

    for m_ref, l_ref, a_ref in ((m1_ref, l1_ref, a1_ref), (m2_ref, l2_ref, a2_ref)):
        m_ref[...] = jnp.full(m_ref.shape, -jnp.inf, F32)
        l_ref[...] = jnp.zeros(l_ref.shape, F32)
        a_ref[...] = jnp.zeros(a_ref.shape, F32)

    def update(s, vt, m_ref, l_ref, a_ref):
        m_old = m_ref[...]
        m_new = jnp.maximum(m_old, jnp.max(s, axis=0, keepdims=True))
        alpha = jnp.exp2((m_old - m_new) * c)
        p = jnp.exp2((s - m_new) * c)
        l_ref[...] = alpha * l_ref[...] + jnp.sum(p, axis=0, keepdims=True)
        a_ref[...] = alpha * a_ref[...] + _mm(vt, p.astype(vt.dtype))
        m_ref[...] = m_new

    def scores(j, buf):
        k = k_ref[pl.ds(pl.multiple_of(j * blk, blk), blk), :]
        buf[0] = _mm(k[:, :hd], qt1)
        buf[1] = _mm(k[:, hd:], qt2)

    def process(j, buf, masked):
        vt = vt_ref[j]
        s1 = buf[0]
        s2 = buf[1]
        if masked:
            r = lax.broadcasted_iota(jnp.int32, s1.shape, 0)
            q = lax.broadcasted_iota(jnp.int32, s1.shape, 1)
            s1 = jnp.where(r <= q, s1, -jnp.inf)
            s2 = jnp.where(r <= q, s2, -jnp.inf)
        update(s1, vt, m1_ref, l1_ref, a1_ref)
        update(s2, vt, m2_ref, l2_ref, a2_ref)

    def body(jj, carry):
        j = 2 * jj
        scores(j + 1, sb_ref)
        process(j, sa_ref, masked=False)
        scores(j + 2, sa_ref)
        process(j + 1, sb_ref, masked=False)
        return carry

    scores(0, sa_ref)
    lax.fori_loop(0, qi // 2, body, 0)

    @pl.when(qi % 2 == 0)
    def _():
        process(qi, sa_ref, masked=True)

    @pl.when(qi % 2 == 1)
    def _():
        scores(qi, sb_ref)
        process(qi - 1, sa_ref, masked=False)
        process(qi, sb_ref, masked=True)

    lp = lam_ref[...]
    lam_init = scal_ref[0]
    lam_full = (jnp.exp(jnp.sum(lp[0:1] * lp[1:2], keepdims=True))
                - jnp.exp(jnp.sum(lp[2:3] * lp[3:4], keepdims=True)) + lam_init)
    o = a1_ref[...] / l1_ref[...] - lam_full * (a2_ref[...] / l2_ref[...])
    y = o * lax.rsqrt(jnp.mean(o * o, axis=0, keepdims=True) + EPS)
    y = (y * subln_ref[...]) * (1.0 - lam_init)
    o_ref[...] = y.T.astype(o_ref.dtype)


def _diff_attention(qkv, lam_l, subln_l, lam_init, B, S, H, hd, blk_pref=512):
    vhd = 2 * hd
    blk = _blk(S, blk_pref)
    qkv3 = qkv.reshape(B, S, 3 * H * vhd)
    scal = jnp.full((1,), lam_init, F32)
    stat = pltpu.VMEM((1, blk), F32)
    acc = pltpu.VMEM((vhd, blk), F32)
    scores = pltpu.VMEM((2, blk, blk), F32)
    return pl.pallas_call(
        functools.partial(_attn_kernel, blk=blk, hd=hd),
        grid=(B, H, S // blk),
        in_specs=[pl.BlockSpec(memory_space=pltpu.SMEM),
                  pl.BlockSpec((4, hd), lambda b, h, i: (0, 0)),
                  pl.BlockSpec((vhd, 1), lambda b, h, i: (0, 0)),
                  pl.BlockSpec((None, blk, vhd), lambda b, h, i: (b, i, h)),
                  pl.BlockSpec((None, S, vhd), lambda b, h, i: (b, 0, H + h)),
                  pl.BlockSpec((None, S, vhd), lambda b, h, i: (b, 0, 2 * H + h))],
        out_specs=pl.BlockSpec((None, blk, vhd), lambda b, h, i: (b, i, h)),
        out_shape=jax.ShapeDtypeStruct((B, S, H * vhd), BF16),
        scratch_shapes=[pltpu.VMEM((S // blk, vhd, blk), BF16), scores, scores,
                        stat, stat, acc, stat, stat, acc],
        compiler_params=_params("parallel", "parallel", "arbitrary"),
        name="diff_attention",
    )(scal, lam_l, subln_l.reshape(vhd, 1), qkv3, qkv3, qkv3).reshape(B * S, H * vhd)


def _sgu_kernel(h_ref, wz_ref, lng_ref, lnb_ref, ws_ref, bs_ref, o_ref, *, chunk, groups):
    W = o_ref.shape[1]
    gd = W // groups
    z = _mm(h_ref[...], wz_ref[...])
    gz = 0.5 * z * (1.0 + lax.erf(z * (2.0 ** -0.5)))
    u = gz[:, :W]
    vg = gz[:, W:]
    mu = jnp.mean(vg, axis=-1, keepdims=True)
    d = vg - mu
    var = jnp.mean(d * d, axis=-1, keepdims=True)
    vn = ((d * lax.rsqrt(var + EPS)) * lng_ref[...] + lnb_ref[...]).astype(BF16)
    r = lax.broadcasted_iota(jnp.int32, (chunk, chunk), 0)
    c = lax.broadcasted_iota(jnp.int32, (chunk, chunk), 1)
    causal = r >= c
    for g in range(groups):
        w = jnp.where(causal, ws_ref[g], jnp.zeros((), ws_ref.dtype))
        bias = bs_ref[:, g:g + 1]
        cols = slice(g * gd, (g + 1) * gd)
        for n in range(o_ref.shape[0] // chunk):
            rows = slice(n * chunk, (n + 1) * chunk)
            mixed = _mm(w, vn[rows, cols]) + bias
            o_ref[rows, cols] = (u[rows, cols] * mixed).astype(o_ref.dtype)


def _sgu(h, wz, ln_g, ln_b, ws, bs_t, chunk, groups):
    T, D = h.shape
    W = wz.shape[1] // 2
    tm = 2 * chunk if T % (2 * chunk) == 0 else chunk
    return pl.pallas_call(
        functools.partial(_sgu_kernel, chunk=chunk, groups=groups),
        grid=(T // tm,),
        in_specs=[pl.BlockSpec((tm, D), lambda i: (i, 0)),
                  pl.BlockSpec((D, 2 * W), lambda i: (0, 0)),
                  pl.BlockSpec((1, W), lambda i: (0, 0)),
                  pl.BlockSpec((1, W), lambda i: (0, 0)),
                  pl.BlockSpec((groups, chunk, chunk), lambda i: (0, 0, 0)),
                  pl.BlockSpec((chunk, groups), lambda i: (0, 0))],
        out_specs=pl.BlockSpec((tm, W), lambda i: (i, 0)),
        out_shape=jax.ShapeDtypeStruct((T, W), BF16),
        compiler_params=_params("parallel"),
        name="sgu",
    )(h, wz, ln_g.reshape(1, W), ln_b.reshape(1, W), ws, bs_t)


def _merge_kernel(h_ref, o_ref, sg_ref, wga_ref, wgs_ref, wa_ref, ws_ref, ba_ref, bs_ref, out_ref):
    h = h_ref[...]
    g_attn = jax.nn.sigmoid(_mm(h, wga_ref[...]) + ba_ref[...])
    g_sgu = jax.nn.sigmoid(_mm(h, wgs_ref[...]) + bs_ref[...])
    y = g_attn * _mm(o_ref[...], wa_ref[...]) + g_sgu * _mm(sg_ref[...], ws_ref[...])
    out_ref[...] = y.astype(out_ref.dtype)


def _merge(h, o, sg, w_gate, gate_b, w_attn, w_sgu):
    T, D = h.shape
    tm, tn = _blk(T, 512), _blk(D, 512)
    nb = D // tn
    row = lambda j, i: (i, 0)
    return pl.pallas_call(
        _merge_kernel,
        grid=(nb, T // tm),
        in_specs=[pl.BlockSpec((tm, D), row),
                  pl.BlockSpec((tm, o.shape[1]), row),
                  pl.BlockSpec((tm, sg.shape[1]), row),
                  pl.BlockSpec((D, tn), lambda j, i: (0, j)),
                  pl.BlockSpec((D, tn), lambda j, i: (0, nb + j)),
                  pl.BlockSpec((w_attn.shape[0], tn), lambda j, i: (0, j)),
                  pl.BlockSpec((w_sgu.shape[0], tn), lambda j, i: (0, j)),
                  pl.BlockSpec((1, tn), lambda j, i: (0, j)),
                  pl.BlockSpec((1, tn), lambda j, i: (0, nb + j))],
        out_specs=pl.BlockSpec((tm, tn), lambda j, i: (i, j)),
        out_shape=jax.ShapeDtypeStruct((T, D), BF16),
        compiler_params=_params("parallel", "arbitrary"),
        name="merge",
    )(h, o, sg, w_gate, w_gate, w_attn, w_sgu, gate_b.reshape(1, -1), gate_b.reshape(1, -1))


def _proj_res_kernel(a_ref, w_ref, x_ref, o_ref):
    o_ref[...] = x_ref[...] + _mm(a_ref[...], w_ref[...])


def _proj_residual(a, w, x):
    T, K = a.shape
    N = w.shape[1]
    tm, tn = _blk(T, 512), _blk(N, 1024)
    return pl.pallas_call(
        _proj_res_kernel,
        grid=(N // tn, T // tm),
        in_specs=[pl.BlockSpec((tm, K), lambda j, i: (i, 0)),
                  pl.BlockSpec((K, tn), lambda j, i: (0, j)),
                  pl.BlockSpec((tm, tn), lambda j, i: (i, j))],
        out_specs=pl.BlockSpec((tm, tn), lambda j, i: (i, j)),
        out_shape=jax.ShapeDtypeStruct((T, N), F32),
        compiler_params=_params("parallel", "arbitrary"),
        name="proj_residual",
    )(a, w, x)


def _ffn_in_kernel(h_ref, wa_ref, wb_ref, o_ref):
    h = h_ref[...]
    a = _mm(h, wa_ref[...])
    b = _mm(h, wb_ref[...])
    o_ref[...] = (a * jax.nn.sigmoid(a) * b).astype(o_ref.dtype)


def _ffn_in(h, w):
    T, D = h.shape
    F = w.shape[1] // 2
    tm, tn = _blk(T, 1024), _blk(F, 512)
    nb = F // tn
    return pl.pallas_call(
        _ffn_in_kernel,
        grid=(nb, T // tm),
        in_specs=[pl.BlockSpec((tm, D), lambda j, i: (i, 0)),
                  pl.BlockSpec((D, tn), lambda j, i: (0, j)),
                  pl.BlockSpec((D, tn), lambda j, i: (0, nb + j))],
        out_specs=pl.BlockSpec((tm, tn), lambda j, i: (i, j)),
        out_shape=jax.ShapeDtypeStruct((T, F), BF16),
        compiler_params=_params("parallel", "arbitrary"),
        name="ffn_in",
    )(h, w, w)


def kernel(x, positions, norm_attn, w_in, gate_b, lam, subln, sgu_ln_g, sgu_ln_b, w_spatial,
           b_spatial, w_branch, w_out, norm_ffn, w_ffn_in, w_ffn_out, norm_final):
    B, S, D = x.shape
    T = B * S
    depth = w_in.shape[0]
    hd = lam.shape[-1]
    H = D // subln.shape[-1]
    groups, chunk = w_spatial.shape[1], w_spatial.shape[2]
    W = sgu_ln_g.shape[-1]
    assert S % chunk == 0 and w_in.shape[2] == 3 * D + 2 * W + 2 * D

    inv_freq = ROPE_THETA ** (-jnp.arange(0, hd, 2, dtype=F32) / hd)
    invf = jnp.concatenate([inv_freq, inv_freq]).reshape(1, hd)
    cos, sin = _rope_tables(positions.reshape(T, 1), invf)

    xf = x.reshape(T, D)
    for l in range(depth):
        lam_init = 0.8 - 0.6 * math.exp(-0.3 * l)
        w_qkv = w_in[l, :, :3 * D].astype(BF16)
        w_z = w_in[l, :, 3 * D:3 * D + 2 * W].astype(BF16)
        w_gate = w_in[l, :, 3 * D + 2 * W:].astype(BF16)

        h = _rmsnorm(xf, norm_attn[l], BF16)
        qkv = _qkv_proj(h, w_qkv, cos, sin, 2 * D)
        o = _diff_attention(qkv, lam[l], subln[l], lam_init, B, S, H, hd)
        sg = _sgu(h, w_z, sgu_ln_g[l], sgu_ln_b[l], w_spatial[l].astype(BF16),
                  b_spatial[l].T, chunk, groups)
        merged = _merge(h, o, sg, w_gate, gate_b[l],
                        w_branch[l, 0].astype(BF16), w_branch[l, 1].astype(BF16))
        xf = _proj_residual(merged, w_out[l].astype(BF16), xf)

        h2 = _rmsnorm(xf, norm_ffn[l], BF16)
        act = _ffn_in(h2, w_ffn_in[l].astype(BF16))
        xf = _proj_residual(act, w_ffn_out[l].astype(BF16), xf)

    return _rmsnorm(xf, norm_final, x.dtype).reshape(B, S, D)
```

```python
import functools
import math

import jax
import jax.numpy as jnp
from jax import lax
from jax.experimental import pallas as pl
from jax.experimental.pallas import tpu as pltpu

ROPE_THETA = 10000.0
EPS = 1e-6
LANES = 128
MXU_COLS = 256
VMEM_LIMIT_BYTES = 56 * 1024 * 1024

F32 = jnp.float32
BF16 = jnp.bfloat16


def _blk(dim, pref):
    if dim <= pref:
        return dim
    b = pref - pref % LANES
    while b > LANES and dim % b:
        b -= LANES
    assert dim % b == 0, (dim, pref)
    return b


def _params(*sem):
    return pltpu.CompilerParams(dimension_semantics=sem, vmem_limit_bytes=VMEM_LIMIT_BYTES)


def _mm(a, b):
    return jnp.dot(a, b, preferred_element_type=F32)


def _rope_kernel(pos_ref, invf_ref, cos_ref, sin_ref):
    ang = pos_ref[...].astype(F32) * invf_ref[...]
    lane = lax.broadcasted_iota(jnp.int32, ang.shape, 1)
    sin = jnp.sin(ang)
    cos_ref[...] = jnp.cos(ang)
    sin_ref[...] = jnp.where(lane < ang.shape[1] // 2, -sin, sin)


def _rope_tables(pos, invf):
    T = pos.shape[0]
    hd = invf.shape[1]
    tm = _blk(T, 2048)
    return pl.pallas_call(
        _rope_kernel,
        grid=(T // tm,),
        in_specs=[pl.BlockSpec((tm, 1), lambda i: (i, 0)),
                  pl.BlockSpec((1, hd), lambda i: (0, 0))],
        out_specs=[pl.BlockSpec((tm, hd), lambda i: (i, 0)),
                   pl.BlockSpec((tm, hd), lambda i: (i, 0))],
        out_shape=[jax.ShapeDtypeStruct((T, hd), F32)] * 2,
        compiler_params=_params("parallel"),
        name="rope_tables",
    )(pos, invf)


def _rmsnorm_kernel(x_ref, g_ref, o_ref):
    x = x_ref[...]
    y = x * lax.rsqrt(jnp.mean(x * x, axis=-1, keepdims=True) + EPS)
    o_ref[...] = (y * g_ref[...]).astype(o_ref.dtype)


def _rmsnorm(x, g, out_dtype):
    T, D = x.shape
    tm = _blk(T, 512)
    return pl.pallas_call(
        _rmsnorm_kernel,
        grid=(T // tm,),
        in_specs=[pl.BlockSpec((tm, D), lambda i: (i, 0)),
                  pl.BlockSpec((1, D), lambda i: (0, 0))],
        out_specs=pl.BlockSpec((tm, D), lambda i: (i, 0)),
        out_shape=jax.ShapeDtypeStruct((T, D), out_dtype),
        compiler_params=_params("parallel"),
        name="rmsnorm",
    )(x, g.reshape(1, D))


def _cache_bf16(w_ref, wb_ref):
    @pl.when(pl.program_id(1) == 0)
    def _():
        wb_ref[...] = w_ref[...].astype(wb_ref.dtype)


def _qk_kernel(h_ref, w_ref, cos_ref, sin_ref, o_ref, wb_ref, *, hd):
    _cache_bf16(w_ref, wb_ref)
    h = h_ref[...]
    cos = cos_ref[...]
    sin = sin_ref[...]
    for c0 in range(0, o_ref.shape[1], MXU_COLS):
        acc = _mm(h, wb_ref[:, c0:c0 + MXU_COLS])
        for s0 in range(0, MXU_COLS, hd):
            t = acc[:, s0:s0 + hd]
            r = t * cos + pltpu.roll(t, hd // 2, 1) * sin
            o_ref[:, c0 + s0:c0 + s0 + hd] = r.astype(o_ref.dtype)


def _v_kernel(h_ref, w_ref, o_ref, wb_ref):
    _cache_bf16(w_ref, wb_ref)
    h = h_ref[...]
    for c0 in range(0, o_ref.shape[1], MXU_COLS):
        o_ref[:, c0:c0 + MXU_COLS] = _mm(h, wb_ref[:, c0:c0 + MXU_COLS]).astype(o_ref.dtype)


def _in_proj(h, w_in, layer, col0, n_cols, cos=None, sin=None):
    T, D = h.shape
    tm, tn = _blk(T, 1024), _blk(n_cols, 1024)
    assert col0 % tn == 0
    c0 = col0 // tn
    in_specs = [pl.BlockSpec((tm, D), lambda j, i: (i, 0)),
                pl.BlockSpec((None, D, tn), lambda j, i: (layer, 0, c0 + j))]
    args = [h, w_in]
    if cos is None:
        body = _v_kernel
    else:
        hd = cos.shape[1]
        body = functools.partial(_qk_kernel, hd=hd)
        in_specs += [pl.BlockSpec((tm, hd), lambda j, i: (i, 0))] * 2
        args += [cos, sin]
    return pl.pallas_call(
        body,
        grid=(n_cols // tn, T // tm),
        in_specs=in_specs,
        out_specs=pl.BlockSpec((tm, tn), lambda j, i: (i, j)),
        out_shape=jax.ShapeDtypeStruct((T, n_cols), BF16),
        scratch_shapes=[pltpu.VMEM((D, tn), BF16)],
        compiler_params=_params("parallel", "arbitrary"),
        name="qk_proj" if cos is not None else "v_proj",
    )(*args)


def _attn_kernel(scal_ref, lam_ref, subln_ref, q_ref, k_ref, v_ref, o_ref,
                 vt_ref, sa_ref, sb_ref, m1_ref, l1_ref, a1_ref, m2_ref, l2_ref, a2_ref,
                 *, blk, hd):
    qi = pl.program_id(2)
    c = hd ** -0.5 * math.log2(math.e)

    @pl.when(qi == 0)
    def _():
        for n in range(vt_ref.shape[0]):
            vt_ref[n] = v_ref[n * blk:(n + 1) * blk, :].T

    qt1 = q_ref[:, :hd].T
    qt2 = q_ref[:, hd:].T

    for m_ref, l_ref, a_ref in ((m1_ref, l1_ref, a1_ref), (m2_ref, l2_ref, a2_ref)):
        m_ref[...] = jnp.full(m_ref.shape, -jnp.inf, F32)
        l_ref[...] = jnp.zeros(l_ref.shape, F32)
        a_ref[...] = jnp.zeros(a_ref.shape, F32)

    def update(s, vt, m_ref, l_ref, a_ref):
        m_old = m_ref[...]
        m_new = jnp.maximum(m_old, jnp.max(s, axis=0, keepdims=True))
        alpha = jnp.exp2((m_old - m_new) * c)
        p = jnp.exp2((s - m_new) * c)
        l_ref[...] = alpha * l_ref[...] + jnp.sum(p, axis=0, keepdims=True)
        a_ref[...] = alpha * a_ref[...] + _mm(vt, p.astype(vt.dtype))
        m_ref[...] = m_new

    def scores(j, buf):
        k = k_ref[pl.ds(pl.multiple_of(j * blk, blk), blk), :]
        buf[0] = _mm(k[:, :hd], qt1)
        buf[1] = _mm(k[:, hd:], qt2)

    def process(j, buf, masked):
        vt = vt_ref[j]
        s1 = buf[0]
        s2 = buf[1]
        if masked:
            r = lax.broadcasted_iota(jnp.int32, s1.shape, 0)
            q = lax.broadcasted_iota(jnp.int32, s1.shape, 1)
            s1 = jnp.where(r <= q, s1, -jnp.inf)
            s2 = jnp.where(r <= q, s2, -jnp.inf)
        update(s1, vt, m1_ref, l1_ref, a1_ref)
        update(s2, vt, m2_ref, l2_ref, a2_ref)

    def body(jj, carry):
        j = 2 * jj
        scores(j + 1, sb_ref)
        process(j, sa_ref, masked=False)
        scores(j + 2, sa_ref)
        process(j + 1, sb_ref, masked=False)
        return carry

    scores(0, sa_ref)
    lax.fori_loop(0, qi // 2, body, 0)

    @pl.when(qi % 2 == 0)
    def _():
        process(qi, sa_ref, masked=True)

    @pl.when(qi % 2 == 1)
    def _():
        scores(qi, sb_ref)
        process(qi - 1, sa_ref, masked=False)
        process(qi, sb_ref, masked=True)

    lp = lam_ref[...]
    lam_init = scal_ref[0]
    lam_full = (jnp.exp(jnp.sum(lp[0:1] * lp[1:2], keepdims=True))
                - jnp.exp(jnp.sum(lp[2:3] * lp[3:4], keepdims=True)) + lam_init)
    o = a1_ref[...] / l1_ref[...] - lam_full * (a2_ref[...] / l2_ref[...])
    y = o * lax.rsqrt(jnp.mean(o * o, axis=0, keepdims=True) + EPS)
    y = (y * subln_ref[...]) * (1.0 - lam_init)
    o_ref[...] = y.T.astype(o_ref.dtype)


def _diff_attention(qk, v, lam_l, subln_l, lam_init, B, S, H, hd, blk_pref=512):
    vhd = 2 * hd
    blk = _blk(S, blk_pref)
    qk3 = qk.reshape(B, S, 2 * H * vhd)
    v3 = v.reshape(B, S, H * vhd)
    scal = jnp.full((1,), lam_init, F32)
    stat = pltpu.VMEM((1, blk), F32)
    acc = pltpu.VMEM((vhd, blk), F32)
    scores = pltpu.VMEM((2, blk, blk), F32)
    return pl.pallas_call(
        functools.partial(_attn_kernel, blk=blk, hd=hd),
        grid=(B, H, S // blk),
        in_specs=[pl.BlockSpec(memory_space=pltpu.SMEM),
                  pl.BlockSpec((4, hd), lambda b, h, i: (0, 0)),
                  pl.BlockSpec((vhd, 1), lambda b, h, i: (0, 0)),
                  pl.BlockSpec((None, blk, vhd), lambda b, h, i: (b, i, h)),
                  pl.BlockSpec((None, S, vhd), lambda b, h, i: (b, 0, H + h)),
                  pl.BlockSpec((None, S, vhd), lambda b, h, i: (b, 0, h))],
        out_specs=pl.BlockSpec((None, blk, vhd), lambda b, h, i: (b, i, h)),
        out_shape=jax.ShapeDtypeStruct((B, S, H * vhd), BF16),
        scratch_shapes=[pltpu.VMEM((S // blk, vhd, blk), BF16), scores, scores,
                        stat, stat, acc, stat, stat, acc],
        compiler_params=_params("parallel", "parallel", "arbitrary"),
        name="diff_attention",
    )(scal, lam_l, subln_l.reshape(vhd, 1), qk3, qk3, v3).reshape(B * S, H * vhd)


def _sgu_kernel(h_ref, wz_ref, lng_ref, lnb_ref, ws_ref, bs_ref, o_ref, mix_ref, *, chunk, groups):
    W = o_ref.shape[1]
    gd = W // groups
    cw = _blk(W, 512)
    h = h_ref[...]

    def gelu_cols(c0):
        z = _mm(h, wz_ref[:, c0:c0 + cw])
        return 0.5 * z * (1.0 + lax.erf(z * (2.0 ** -0.5)))

    vg = jnp.concatenate([gelu_cols(W + c0) for c0 in range(0, W, cw)], axis=1)
    mu = jnp.mean(vg, axis=-1, keepdims=True)
    d = vg - mu
    var = jnp.mean(d * d, axis=-1, keepdims=True)
    vn = ((d * lax.rsqrt(var + EPS)) * lng_ref[...] + lnb_ref[...]).astype(BF16)
    r = lax.broadcasted_iota(jnp.int32, (chunk, chunk), 0)
    c = lax.broadcasted_iota(jnp.int32, (chunk, chunk), 1)
    causal = r >= c
    for g in range(groups):
        w = jnp.where(causal, ws_ref[g], jnp.zeros((), ws_ref.dtype))
        bias = bs_ref[:, g:g + 1]
        cols = slice(g * gd, (g + 1) * gd)
        for n in range(o_ref.shape[0] // chunk):
            rows = slice(n * chunk, (n + 1) * chunk)
            mix_ref[rows, cols] = _mm(w, vn[rows, cols]) + bias
    for c0 in range(0, W, cw):
        o_ref[:, c0:c0 + cw] = (gelu_cols(c0) * mix_ref[:, c0:c0 + cw]).astype(o_ref.dtype)


def _sgu(h, wz, ln_g, ln_b, ws, bs_t, chunk, groups):
    T, D = h.shape
    W = wz.shape[1] // 2
    tm = 2 * chunk if T % (2 * chunk) == 0 else chunk
    return pl.pallas_call(
        functools.partial(_sgu_kernel, chunk=chunk, groups=groups),
        grid=(T // tm,),
        in_specs=[pl.BlockSpec((tm, D), lambda i: (i, 0)),
                  pl.BlockSpec((D, 2 * W), lambda i: (0, 0)),
                  pl.BlockSpec((1, W), lambda i: (0, 0)),
                  pl.BlockSpec((1, W), lambda i: (0, 0)),
                  pl.BlockSpec((groups, chunk, chunk), lambda i: (0, 0, 0)),
                  pl.BlockSpec((chunk, groups), lambda i: (0, 0))],
        out_specs=pl.BlockSpec((tm, W), lambda i: (i, 0)),
        out_shape=jax.ShapeDtypeStruct((T, W), BF16),
        scratch_shapes=[pltpu.VMEM((tm, W), F32)],
        compiler_params=_params("parallel"),
        name="sgu",
    )(h, wz, ln_g.reshape(1, W), ln_b.reshape(1, W), ws, bs_t)


def _merge_kernel(h_ref, o_ref, sg_ref, wga_ref, wgs_ref, wa_ref, ws_ref, ba_ref, bs_ref, out_ref):
    h = h_ref[...]
    g_attn = jax.nn.sigmoid(_mm(h, wga_ref[...]) + ba_ref[...])
    g_sgu = jax.nn.sigmoid(_mm(h, wgs_ref[...]) + bs_ref[...])
    y = g_attn * _mm(o_ref[...], wa_ref[...]) + g_sgu * _mm(sg_ref[...], ws_ref[...])
    out_ref[...] = y.astype(out_ref.dtype)


def _merge(h, o, sg, w_gate, gate_b, w_attn, w_sgu):
    T, D = h.shape
    tm, tn = _blk(T, 512), _blk(D, 512)
    nb = D // tn
    row = lambda j, i: (i, 0)
    return pl.pallas_call(
        _merge_kernel,
        grid=(nb, T // tm),
        in_specs=[pl.BlockSpec((tm, D), row),
                  pl.BlockSpec((tm, o.shape[1]), row),
                  pl.BlockSpec((tm, sg.shape[1]), row),
                  pl.BlockSpec((D, tn), lambda j, i: (0, j)),
                  pl.BlockSpec((D, tn), lambda j, i: (0, nb + j)),
                  pl.BlockSpec((w_attn.shape[0], tn), lambda j, i: (0, j)),
                  pl.BlockSpec((w_sgu.shape[0], tn), lambda j, i: (0, j)),
                  pl.BlockSpec((1, tn), lambda j, i: (0, j)),
                  pl.BlockSpec((1, tn), lambda j, i: (0, nb + j))],
        out_specs=pl.BlockSpec((tm, tn), lambda j, i: (i, j)),
        out_shape=jax.ShapeDtypeStruct((T, D), BF16),
        compiler_params=_params("parallel", "arbitrary"),
        name="merge",
    )(h, o, sg, w_gate, w_gate, w_attn, w_sgu, gate_b.reshape(1, -1), gate_b.reshape(1, -1))


def _proj_res_norm_kernel(a_ref, w_ref, x_ref, g_ref, *out_refs):
    y = x_ref[...] + _mm(a_ref[...], w_ref[...])
    if len(out_refs) == 2:
        out_refs[0][...] = y
    n = y * lax.rsqrt(jnp.mean(y * y, axis=-1, keepdims=True) + EPS)
    out_refs[-1][...] = (n * g_ref[...]).astype(out_refs[-1].dtype)


def _proj_res_norm(a, w, x, g, norm_dtype, emit_x=True):
    T, K = a.shape
    D = w.shape[1]
    tm = _blk(T, 512 if K <= D else 256)
    row = lambda i: (i, 0)
    out_specs = [pl.BlockSpec((tm, D), row)] * (2 if emit_x else 1)
    out_shape = ([jax.ShapeDtypeStruct((T, D), F32)] if emit_x else []) + [
        jax.ShapeDtypeStruct((T, D), norm_dtype)]
    outs = pl.pallas_call(
        _proj_res_norm_kernel,
        grid=(T // tm,),
        in_specs=[pl.BlockSpec((tm, K), row),
                  pl.BlockSpec((K, D), lambda i: (0, 0), pipeline_mode=pl.Buffered(1)),
                  pl.BlockSpec((tm, D), row),
                  pl.BlockSpec((1, D), lambda i: (0, 0))],
        out_specs=out_specs,
        out_shape=out_shape,
        compiler_params=_params("parallel"),
        name="proj_res_norm",
    )(a, w, x, g.reshape(1, D))
    return outs if emit_x else outs[0]


def _ffn_in_kernel(h_ref, wa_ref, wb_ref, o_ref, wab_ref, wbb_ref):
    _cache_bf16(wa_ref, wab_ref)
    _cache_bf16(wb_ref, wbb_ref)
    h = h_ref[...]
    a = _mm(h, wab_ref[...])
    b = _mm(h, wbb_ref[...])
    o_ref[...] = (a * jax.nn.sigmoid(a) * b).astype(o_ref.dtype)


def _ffn_in(h, w_ffn_in, layer):
    T, D = h.shape
    F = w_ffn_in.shape[2] // 2
    tm, tn = _blk(T, 1024), _blk(F, 512)
    nb = F // tn
    return pl.pallas_call(
        _ffn_in_kernel,
        grid=(nb, T // tm),
        in_specs=[pl.BlockSpec((tm, D), lambda j, i: (i, 0)),
                  pl.BlockSpec((None, D, tn), lambda j, i: (layer, 0, j)),
                  pl.BlockSpec((None, D, tn), lambda j, i: (layer, 0, nb + j))],
        out_specs=pl.BlockSpec((tm, tn), lambda j, i: (i, j)),
        out_shape=jax.ShapeDtypeStruct((T, F), BF16),
        scratch_shapes=[pltpu.VMEM((D, tn), BF16)] * 2,
        compiler_params=_params("parallel", "arbitrary"),
        name="ffn_in",
    )(h, w_ffn_in, w_ffn_in)


def kernel(x, positions, norm_attn, w_in, gate_b, lam, subln, sgu_ln_g, sgu_ln_b, w_spatial,
           b_spatial, w_branch, w_out, norm_ffn, w_ffn_in, w_ffn_out, norm_final):
    B, S, D = x.shape
    T = B * S
    depth = w_in.shape[0]
    hd = lam.shape[-1]
    H = D // subln.shape[-1]
    groups, chunk = w_spatial.shape[1], w_spatial.shape[2]
    W = sgu_ln_g.shape[-1]
    assert S % chunk == 0 and w_in.shape[2] == 3 * D + 2 * W + 2 * D

    inv_freq = ROPE_THETA ** (-jnp.arange(0, hd, 2, dtype=F32) / hd)
    invf = jnp.concatenate([inv_freq, inv_freq]).reshape(1, hd)
    cos, sin = _rope_tables(positions.reshape(T, 1), invf)

    xf = x.reshape(T, D)
    h = _rmsnorm(xf, norm_attn[0], BF16)
    for l in range(depth):
        lam_init = 0.8 - 0.6 * math.exp(-0.3 * l)
        w_z = w_in[l, :, 3 * D:3 * D + 2 * W].astype(BF16)
        w_gate = w_in[l, :, 3 * D + 2 * W:].astype(BF16)

        qk = _in_proj(h, w_in, l, 0, 2 * D, cos, sin)
        v = _in_proj(h, w_in, l, 2 * D, D)
        o = _diff_attention(qk, v, lam[l], subln[l], lam_init, B, S, H, hd)
        sg = _sgu(h, w_z, sgu_ln_g[l], sgu_ln_b[l], w_spatial[l].astype(BF16),
                  b_spatial[l].T, chunk, groups)
        merged = _merge(h, o, sg, w_gate, gate_b[l],
                        w_branch[l, 0].astype(BF16), w_branch[l, 1].astype(BF16))
        xf, h2 = _proj_res_norm(merged, w_out[l].astype(BF16), xf, norm_ffn[l], BF16)

        act = _ffn_in(h2, w_ffn_in, l)
        if l + 1 < depth:
            xf, h = _proj_res_norm(act, w_ffn_out[l].astype(BF16), xf, norm_attn[l + 1], BF16)
        else:
            out = _proj_res_norm(act, w_ffn_out[l].astype(BF16), xf, norm_final, x.dtype,
                                 emit_x=False)
    return out.reshape(B, S, D)
```

```python
import functools
import math

import jax
import jax.numpy as jnp
from jax import lax
from jax.experimental import pallas as pl
from jax.experimental.pallas import tpu as pltpu

ROPE_THETA = 10000.0
EPS = 1e-6
LANES = 128
MXU_COLS = 256
VMEM_LIMIT_BYTES = 56 * 1024 * 1024

F32 = jnp.float32
BF16 = jnp.bfloat16


def _blk(dim, pref):
    if dim <= pref:
        return dim
    b = pref - pref % LANES
    while b > LANES and dim % b:
        b -= LANES
    assert dim % b == 0, (dim, pref)
    return b


def _params(*sem):
    return pltpu.CompilerParams(dimension_semantics=sem, vmem_limit_bytes=VMEM_LIMIT_BYTES)


def _mm(a, b):
    return jnp.dot(a, b, preferred_element_type=F32)


def _rope_kernel(pos_ref, invf_ref, cos_ref, sin_ref):
    ang = pos_ref[...].astype(F32) * invf_ref[...]
    lane = lax.broadcasted_iota(jnp.int32, ang.shape, 1)
    sin = jnp.sin(ang)
    cos_ref[...] = jnp.cos(ang)
    sin_ref[...] = jnp.where(lane < ang.shape[1] // 2, -sin, sin)


def _rope_tables(pos, invf):
    T = pos.shape[0]
    hd = invf.shape[1]
    tm = _blk(T, 2048)
    return pl.pallas_call(
        _rope_kernel,
        grid=(T // tm,),
        in_specs=[pl.BlockSpec((tm, 1), lambda i: (i, 0)),
                  pl.BlockSpec((1, hd), lambda i: (0, 0))],
        out_specs=[pl.BlockSpec((tm, hd), lambda i: (i, 0)),
                   pl.BlockSpec((tm, hd), lambda i: (i, 0))],
        out_shape=[jax.ShapeDtypeStruct((T, hd), F32)] * 2,
        compiler_params=_params("parallel"),
        name="rope_tables",
    )(pos, invf)


def _rmsnorm_kernel(x_ref, g_ref, o_ref):
    x = x_ref[...]
    y = x * lax.rsqrt(jnp.mean(x * x, axis=-1, keepdims=True) + EPS)
    o_ref[...] = (y * g_ref[...]).astype(o_ref.dtype)


def _rmsnorm(x, g, out_dtype):
    T, D = x.shape
    tm = _blk(T, 512)
    return pl.pallas_call(
        _rmsnorm_kernel,
        grid=(T // tm,),
        in_specs=[pl.BlockSpec((tm, D), lambda i: (i, 0)),
                  pl.BlockSpec((1, D), lambda i: (0, 0))],
        out_specs=pl.BlockSpec((tm, D), lambda i: (i, 0)),
        out_shape=jax.ShapeDtypeStruct((T, D), out_dtype),
        compiler_params=_params("parallel"),
        name="rmsnorm",
    )(x, g.reshape(1, D))


def _cache_bf16(w_ref, wb_ref):
    @pl.when(pl.program_id(1) == 0)
    def _():
        wb_ref[...] = w_ref[...].astype(wb_ref.dtype)


def _qk_kernel(h_ref, w_ref, cos_ref, sin_ref, o_ref, wb_ref, *, hd):
    _cache_bf16(w_ref, wb_ref)
    h = h_ref[...]
    cos = cos_ref[...]
    sin = sin_ref[...]
    for c0 in range(0, o_ref.shape[1], MXU_COLS):
        acc = _mm(h, wb_ref[:, c0:c0 + MXU_COLS])
        for s0 in range(0, MXU_COLS, hd):
            t = acc[:, s0:s0 + hd]
            r = t * cos + pltpu.roll(t, hd // 2, 1) * sin
            o_ref[:, c0 + s0:c0 + s0 + hd] = r.astype(o_ref.dtype)


def _v_kernel(h_ref, w_ref, o_ref, wb_ref):
    _cache_bf16(w_ref, wb_ref)
    h = h_ref[...]
    for c0 in range(0, o_ref.shape[1], MXU_COLS):
        o_ref[:, c0:c0 + MXU_COLS] = _mm(h, wb_ref[:, c0:c0 + MXU_COLS]).astype(o_ref.dtype)


def _in_proj(h, w_in, layer, col0, n_cols, cos=None, sin=None):
    T, D = h.shape
    tm, tn = _blk(T, 1024), _blk(n_cols, 1024)
    assert col0 % tn == 0
    c0 = col0 // tn
    in_specs = [pl.BlockSpec((tm, D), lambda j, i: (i, 0)),
                pl.BlockSpec((None, D, tn), lambda j, i: (layer, 0, c0 + j))]
    args = [h, w_in]
    if cos is None:
        body = _v_kernel
    else:
        hd = cos.shape[1]
        body = functools.partial(_qk_kernel, hd=hd)
        in_specs += [pl.BlockSpec((tm, hd), lambda j, i: (i, 0))] * 2
        args += [cos, sin]
    return pl.pallas_call(
        body,
        grid=(n_cols // tn, T // tm),
        in_specs=in_specs,
        out_specs=pl.BlockSpec((tm, tn), lambda j, i: (i, j)),
        out_shape=jax.ShapeDtypeStruct((T, n_cols), BF16),
        scratch_shapes=[pltpu.VMEM((D, tn), BF16)],
        compiler_params=_params("parallel", "arbitrary"),
        name="qk_proj" if cos is not None else "v_proj",
    )(*args)


def _attn_kernel(scal_ref, lam_ref, subln_ref, q_ref, k_ref, v_ref, o_ref,
                 vt_ref, qt_ref, sa_ref, sb_ref, m1_ref, l1_ref, a1_ref, m2_ref, l2_ref, a2_ref,
                 *, blk, hd):
    nq = vt_ref.shape[0]
    unroll = 4 if nq % 4 == 0 else nq
    c = hd ** -0.5 * math.log2(math.e)

    for n in range(nq):
        rows = slice(n * blk, (n + 1) * blk)
        vt_ref[n] = v_ref[rows, :].T
        qt_ref[n, 0] = q_ref[rows, :hd].T
        qt_ref[n, 1] = q_ref[rows, hd:].T

    lp = lam_ref[...]
    lam_init = scal_ref[0]
    lam_full = (jnp.exp(jnp.sum(lp[0:1] * lp[1:2], keepdims=True))
                - jnp.exp(jnp.sum(lp[2:3] * lp[3:4], keepdims=True)) + lam_init)

    def update(s, vt, m_ref, l_ref, a_ref):
        m_old = m_ref[...]
        m_new = jnp.maximum(m_old, jnp.max(s, axis=0, keepdims=True))
        alpha = jnp.exp2((m_old - m_new) * c)
        p = jnp.exp2((s - m_new) * c)
        l_ref[...] = alpha * l_ref[...] + jnp.sum(p, axis=0, keepdims=True)
        a_ref[...] = alpha * a_ref[...] + _mm(vt, p.astype(vt.dtype))
        m_ref[...] = m_new

    def init_stats():
        for m_ref, l_ref, a_ref in ((m1_ref, l1_ref, a1_ref), (m2_ref, l2_ref, a2_ref)):
            m_ref[...] = jnp.full(m_ref.shape, -jnp.inf, F32)
            l_ref[...] = jnp.zeros(l_ref.shape, F32)
            a_ref[...] = jnp.zeros(a_ref.shape, F32)

    def scores(qi, j, buf):
        k = k_ref[pl.ds(pl.multiple_of(j * blk, blk), blk), :]
        buf[0] = _mm(k[:, :hd], qt_ref[qi, 0])
        buf[1] = _mm(k[:, hd:], qt_ref[qi, 1])

    def process(j, buf, masked):
        vt = vt_ref[j]
        s1 = buf[0]
        s2 = buf[1]
        if masked:
            r = lax.broadcasted_iota(jnp.int32, s1.shape, 0)
            q = lax.broadcasted_iota(jnp.int32, s1.shape, 1)
            s1 = jnp.where(r <= q, s1, -jnp.inf)
            s2 = jnp.where(r <= q, s2, -jnp.inf)
        update(s1, vt, m1_ref, l1_ref, a1_ref)
        update(s2, vt, m2_ref, l2_ref, a2_ref)

    def finish(qi):
        o = a1_ref[...] / l1_ref[...] - lam_full * (a2_ref[...] / l2_ref[...])
        init_stats()
        y = o * lax.rsqrt(jnp.mean(o * o, axis=0, keepdims=True) + EPS)
        y = (y * subln_ref[...]) * (1.0 - lam_init)
        o_ref[pl.ds(pl.multiple_of(qi * blk, blk), blk), :] = y.T.astype(o_ref.dtype)

    def q_block(qi, odd, cur, oth):
        def body(jj, carry):
            j = 2 * jj
            scores(qi, j + 1, oth)
            process(j, cur, masked=False)
            scores(qi, j + 2, cur)
            process(j + 1, oth, masked=False)
            return carry

        lax.fori_loop(0, qi // 2, body, 0)
        nxt = jnp.minimum(qi + 1, nq - 1)
        if odd:
            scores(qi, qi, oth)
            process(qi - 1, cur, masked=False)
            scores(nxt, 0, cur)
            process(qi, oth, masked=True)
            first = cur
        else:
            scores(nxt, 0, oth)
            process(qi, cur, masked=True)
            first = oth
        finish(qi)
        return first

    def q_group(g, carry):
        cur = sa_ref
        for r in range(unroll):
            oth = sb_ref if cur is sa_ref else sa_ref
            cur = q_block(g * unroll + r, r % 2 == 1, cur, oth)
        assert cur is sa_ref or nq == unroll
        return carry

    init_stats()
    scores(0, 0, sa_ref)
    if nq == unroll:
        q_group(0, 0)
    else:
        lax.fori_loop(0, nq // unroll, q_group, 0)


def _diff_attention(qk, v, lam_l, subln_l, lam_init, B, S, H, hd, blk_pref=512):
    vhd = 2 * hd
    blk = _blk(S, blk_pref)
    qk3 = qk.reshape(B, S, 2 * H * vhd)
    v3 = v.reshape(B, S, H * vhd)
    scal = jnp.full((1,), lam_init, F32)
    stat = pltpu.VMEM((1, blk), F32)
    acc = pltpu.VMEM((vhd, blk), F32)
    scores = pltpu.VMEM((2, blk, blk), F32)
    nq = S // blk
    return pl.pallas_call(
        functools.partial(_attn_kernel, blk=blk, hd=hd),
        grid=(B, H),
        in_specs=[pl.BlockSpec(memory_space=pltpu.SMEM),
                  pl.BlockSpec((4, hd), lambda b, h: (0, 0)),
                  pl.BlockSpec((vhd, 1), lambda b, h: (0, 0)),
                  pl.BlockSpec((None, S, vhd), lambda b, h: (b, 0, h)),
                  pl.BlockSpec((None, S, vhd), lambda b, h: (b, 0, H + h)),
                  pl.BlockSpec((None, S, vhd), lambda b, h: (b, 0, h))],
        out_specs=pl.BlockSpec((None, S, vhd), lambda b, h: (b, 0, h)),
        out_shape=jax.ShapeDtypeStruct((B, S, H * vhd), BF16),
        scratch_shapes=[pltpu.VMEM((nq, vhd, blk), BF16), pltpu.VMEM((nq, 2, hd, blk), BF16),
                        scores, scores, stat, stat, acc, stat, stat, acc],
        compiler_params=_params("parallel", "parallel"),
        name="diff_attention",
    )(scal, lam_l, subln_l.reshape(vhd, 1), qk3, qk3, v3).reshape(B * S, H * vhd)


def _sgu_kernel(h_ref, wz_ref, lng_ref, lnb_ref, ws_ref, bs_ref, o_ref, u_ref, *, chunk, groups):
    W = o_ref.shape[1]
    gd = W // groups
    cw = _blk(W, 512)
    h = h_ref[...]

    def gelu_cols(c0):
        z = _mm(h, wz_ref[:, c0:c0 + cw])
        return 0.5 * z * (1.0 + lax.erf(z * (2.0 ** -0.5)))

    vg = jnp.concatenate([gelu_cols(W + c0) for c0 in range(0, W, cw)], axis=1)
    mu = jnp.mean(vg, axis=-1, keepdims=True)
    d = vg - mu
    var = jnp.mean(d * d, axis=-1, keepdims=True)
    vn = ((d * lax.rsqrt(var + EPS)) * lng_ref[...] + lnb_ref[...]).astype(BF16)
    for c0 in range(0, W, cw):
        u_ref[:, c0:c0 + cw] = gelu_cols(c0)
    r = lax.broadcasted_iota(jnp.int32, (chunk, chunk), 0)
    c = lax.broadcasted_iota(jnp.int32, (chunk, chunk), 1)
    causal = r >= c
    for g in range(groups):
        w = jnp.where(causal, ws_ref[g], jnp.zeros((), ws_ref.dtype))
        bias = bs_ref[:, g:g + 1]
        cols = slice(g * gd, (g + 1) * gd)
        for n in range(o_ref.shape[0] // chunk):
            rows = slice(n * chunk, (n + 1) * chunk)
            mixed = _mm(w, vn[rows, cols]) + bias
            o_ref[rows, cols] = (u_ref[rows, cols] * mixed).astype(o_ref.dtype)


def _sgu(h, wz, ln_g, ln_b, ws, bs_t, chunk, groups):
    T, D = h.shape
    W = wz.shape[1] // 2
    tm = 2 * chunk if T % (2 * chunk) == 0 else chunk
    return pl.pallas_call(
        functools.partial(_sgu_kernel, chunk=chunk, groups=groups),
        grid=(T // tm,),
        in_specs=[pl.BlockSpec((tm, D), lambda i: (i, 0)),
                  pl.BlockSpec((D, 2 * W), lambda i: (0, 0)),
                  pl.BlockSpec((1, W), lambda i: (0, 0)),
                  pl.BlockSpec((1, W), lambda i: (0, 0)),
                  pl.BlockSpec((groups, chunk, chunk), lambda i: (0, 0, 0)),
                  pl.BlockSpec((chunk, groups), lambda i: (0, 0))],
        out_specs=pl.BlockSpec((tm, W), lambda i: (i, 0)),
        out_shape=jax.ShapeDtypeStruct((T, W), BF16),
        scratch_shapes=[pltpu.VMEM((tm, W), F32)],
        compiler_params=_params("parallel"),
        name="sgu",
    )(h, wz, ln_g.reshape(1, W), ln_b.reshape(1, W), ws, bs_t)


def _merge_kernel(h_ref, o_ref, sg_ref, wga_ref, wgs_ref, wa_ref, ws_ref, ba_ref, bs_ref, out_ref):
    h = h_ref[...]
    g_attn = jax.nn.sigmoid(_mm(h, wga_ref[...]) + ba_ref[...])
    g_sgu = jax.nn.sigmoid(_mm(h, wgs_ref[...]) + bs_ref[...])
    y = g_attn * _mm(o_ref[...], wa_ref[...]) + g_sgu * _mm(sg_ref[...], ws_ref[...])
    out_ref[...] = y.astype(out_ref.dtype)


def _merge(h, o, sg, w_gate, gate_b, w_attn, w_sgu):
    T, D = h.shape
    tm, tn = _blk(T, 512), _blk(D, 512)
    nb = D // tn
    row = lambda j, i: (i, 0)
    return pl.pallas_call(
        _merge_kernel,
        grid=(nb, T // tm),
        in_specs=[pl.BlockSpec((tm, D), row),
                  pl.BlockSpec((tm, o.shape[1]), row),
                  pl.BlockSpec((tm, sg.shape[1]), row),
                  pl.BlockSpec((D, tn), lambda j, i: (0, j)),
                  pl.BlockSpec((D, tn), lambda j, i: (0, nb + j)),
                  pl.BlockSpec((w_attn.shape[0], tn), lambda j, i: (0, j)),
                  pl.BlockSpec((w_sgu.shape[0], tn), lambda j, i: (0, j)),
                  pl.BlockSpec((1, tn), lambda j, i: (0, j)),
                  pl.BlockSpec((1, tn), lambda j, i: (0, nb + j))],
        out_specs=pl.BlockSpec((tm, tn), lambda j, i: (i, j)),
        out_shape=jax.ShapeDtypeStruct((T, D), BF16),
        compiler_params=_params("parallel", "arbitrary"),
        name="merge",
    )(h, o, sg, w_gate, w_gate, w_attn, w_sgu, gate_b.reshape(1, -1), gate_b.reshape(1, -1))


def _proj_res_norm_kernel(a_ref, w_ref, x_ref, g_ref, *out_refs):
    y = x_ref[...] + _mm(a_ref[...], w_ref[...])
    if len(out_refs) == 2:
        out_refs[0][...] = y
    n = y * lax.rsqrt(jnp.mean(y * y, axis=-1, keepdims=True) + EPS)
    out_refs[-1][...] = (n * g_ref[...]).astype(out_refs[-1].dtype)


def _proj_res_norm(a, w, x, g, norm_dtype, emit_x=True):
    T, K = a.shape
    D = w.shape[1]
    tm = _blk(T, 512 if K <= D else 256)
    row = lambda i: (i, 0)
    out_specs = [pl.BlockSpec((tm, D), row)] * (2 if emit_x else 1)
    out_shape = ([jax.ShapeDtypeStruct((T, D), F32)] if emit_x else []) + [
        jax.ShapeDtypeStruct((T, D), norm_dtype)]
    outs = pl.pallas_call(
        _proj_res_norm_kernel,
        grid=(T // tm,),
        in_specs=[pl.BlockSpec((tm, K), row),
                  pl.BlockSpec((K, D), lambda i: (0, 0), pipeline_mode=pl.Buffered(1)),
                  pl.BlockSpec((tm, D), row),
                  pl.BlockSpec((1, D), lambda i: (0, 0))],
        out_specs=out_specs,
        out_shape=out_shape,
        compiler_params=_params("parallel"),
        name="proj_res_norm",
    )(a, w, x, g.reshape(1, D))
    return outs if emit_x else outs[0]


def _ffn_in_kernel(h_ref, wa_ref, wb_ref, o_ref, wab_ref, wbb_ref):
    _cache_bf16(wa_ref, wab_ref)
    _cache_bf16(wb_ref, wbb_ref)
    h = h_ref[...]
    a = _mm(h, wab_ref[...])
    b = _mm(h, wbb_ref[...])
    o_ref[...] = (a * jax.nn.sigmoid(a) * b).astype(o_ref.dtype)


def _ffn_in(h, w_ffn_in, layer):
    T, D = h.shape
    F = w_ffn_in.shape[2] // 2
    tm, tn = _blk(T, 1024), _blk(F, 512)
    nb = F // tn
    return pl.pallas_call(
        _ffn_in_kernel,
        grid=(nb, T // tm),
        in_specs=[pl.BlockSpec((tm, D), lambda j, i: (i, 0)),
                  pl.BlockSpec((None, D, tn), lambda j, i: (layer, 0, j)),
                  pl.BlockSpec((None, D, tn), lambda j, i: (layer, 0, nb + j))],
        out_specs=pl.BlockSpec((tm, tn), lambda j, i: (i, j)),
        out_shape=jax.ShapeDtypeStruct((T, F), BF16),
        scratch_shapes=[pltpu.VMEM((D, tn), BF16)] * 2,
        compiler_params=_params("parallel", "arbitrary"),
        name="ffn_in",
    )(h, w_ffn_in, w_ffn_in)


def kernel(x, positions, norm_attn, w_in, gate_b, lam, subln, sgu_ln_g, sgu_ln_b, w_spatial,
           b_spatial, w_branch, w_out, norm_ffn, w_ffn_in, w_ffn_out, norm_final):
    B, S, D = x.shape
    T = B * S
    depth = w_in.shape[0]
    hd = lam.shape[-1]
    H = D // subln.shape[-1]
    groups, chunk = w_spatial.shape[1], w_spatial.shape[2]
    W = sgu_ln_g.shape[-1]
    assert S % chunk == 0 and w_in.shape[2] == 3 * D + 2 * W + 2 * D

    inv_freq = ROPE_THETA ** (-jnp.arange(0, hd, 2, dtype=F32) / hd)
    invf = jnp.concatenate([inv_freq, inv_freq]).reshape(1, hd)
    cos, sin = _rope_tables(positions.reshape(T, 1), invf)

    xf = x.reshape(T, D)
    h = _rmsnorm(xf, norm_attn[0], BF16)
    for l in range(depth):
        lam_init = 0.8 - 0.6 * math.exp(-0.3 * l)
        w_z = w_in[l, :, 3 * D:3 * D + 2 * W].astype(BF16)
        w_gate = w_in[l, :, 3 * D + 2 * W:].astype(BF16)

        qk = _in_proj(h, w_in, l, 0, 2 * D, cos, sin)
        v = _in_proj(h, w_in, l, 2 * D, D)
        o = _diff_attention(qk, v, lam[l], subln[l], lam_init, B, S, H, hd)
        sg = _sgu(h, w_z, sgu_ln_g[l], sgu_ln_b[l], w_spatial[l].astype(BF16),
                  b_spatial[l].T, chunk, groups)
        merged = _merge(h, o, sg, w_gate, gate_b[l],
                        w_branch[l, 0].astype(BF16), w_branch[l, 1].astype(BF16))
        xf, h2 = _proj_res_norm(merged, w_out[l].astype(BF16), xf, norm_ffn[l], BF16)

        act = _ffn_in(h2, w_ffn_in, l)
        if l + 1 < depth:
            xf, h = _proj_res_norm(act, w_ffn_out[l].astype(BF16), xf, norm_attn[l + 1], BF16)
        else:
            out = _proj_res_norm(act, w_ffn_out[l].astype(BF16), xf, norm_final, x.dtype,
                                 emit_x=False)
    return out.reshape(B, S, D)
```

```python
import functools
import math

import jax
import jax.numpy as jnp
from jax import lax
from jax.experimental import pallas as pl
from jax.experimental.pallas import tpu as pltpu

ROPE_THETA = 10000.0
EPS = 1e-6
LANES = 128
MXU_COLS = 256
VMEM_LIMIT_BYTES = 56 * 1024 * 1024

F32 = jnp.float32
BF16 = jnp.bfloat16


def _blk(dim, pref):
    if dim <= pref:
        return dim
    b = pref - pref % LANES
    while b > LANES and dim % b:
        b -= LANES
    assert dim % b == 0, (dim, pref)
    return b


def _params(*sem):
    return pltpu.CompilerParams(dimension_semantics=sem, vmem_limit_bytes=VMEM_LIMIT_BYTES)


def _mm(a, b):
    return jnp.dot(a, b, preferred_element_type=F32)


def _rope_kernel(pos_ref, invf_ref, cos_ref, sin_ref):
    ang = pos_ref[...].astype(F32) * invf_ref[...]
    lane = lax.broadcasted_iota(jnp.int32, ang.shape, 1)
    sin = jnp.sin(ang)
    cos_ref[...] = jnp.cos(ang)
    sin_ref[...] = jnp.where(lane < ang.shape[1] // 2, -sin, sin)


def _rope_tables(pos, invf):
    T = pos.shape[0]
    hd = invf.shape[1]
    tm = _blk(T, 2048)
    return pl.pallas_call(
        _rope_kernel,
        grid=(T // tm,),
        in_specs=[pl.BlockSpec((tm, 1), lambda i: (i, 0)),
                  pl.BlockSpec((1, hd), lambda i: (0, 0))],
        out_specs=[pl.BlockSpec((tm, hd), lambda i: (i, 0)),
                   pl.BlockSpec((tm, hd), lambda i: (i, 0))],
        out_shape=[jax.ShapeDtypeStruct((T, hd), F32)] * 2,
        compiler_params=_params("parallel"),
        name="rope_tables",
    )(pos, invf)


def _rmsnorm_kernel(x_ref, g_ref, o_ref):
    x = x_ref[...]
    y = x * lax.rsqrt(jnp.mean(x * x, axis=-1, keepdims=True) + EPS)
    o_ref[...] = (y * g_ref[...]).astype(o_ref.dtype)


def _rmsnorm(x, g, out_dtype):
    T, D = x.shape
    tm = _blk(T, 512)
    return pl.pallas_call(
        _rmsnorm_kernel,
        grid=(T // tm,),
        in_specs=[pl.BlockSpec((tm, D), lambda i: (i, 0)),
                  pl.BlockSpec((1, D), lambda i: (0, 0))],
        out_specs=pl.BlockSpec((tm, D), lambda i: (i, 0)),
        out_shape=jax.ShapeDtypeStruct((T, D), out_dtype),
        compiler_params=_params("parallel"),
        name="rmsnorm",
    )(x, g.reshape(1, D))


def _cache_bf16(w_ref, wb_ref):
    @pl.when(pl.program_id(1) == 0)
    def _():
        wb_ref[...] = w_ref[...].astype(wb_ref.dtype)


def _qk_kernel(h_ref, w_ref, cos_ref, sin_ref, o_ref, wb_ref, *, hd):
    _cache_bf16(w_ref, wb_ref)
    h = h_ref[...]
    cos = cos_ref[...]
    sin = sin_ref[...]
    for c0 in range(0, o_ref.shape[1], MXU_COLS):
        acc = _mm(h, wb_ref[:, c0:c0 + MXU_COLS])
        for s0 in range(0, MXU_COLS, hd):
            t = acc[:, s0:s0 + hd]
            r = t * cos + pltpu.roll(t, hd // 2, 1) * sin
            o_ref[:, c0 + s0:c0 + s0 + hd] = r.astype(o_ref.dtype)


def _v_kernel(h_ref, w_ref, o_ref, wb_ref):
    _cache_bf16(w_ref, wb_ref)
    h = h_ref[...]
    for c0 in range(0, o_ref.shape[1], MXU_COLS):
        o_ref[:, c0:c0 + MXU_COLS] = _mm(h, wb_ref[:, c0:c0 + MXU_COLS]).astype(o_ref.dtype)


def _in_proj(h, w_in, layer, col0, n_cols, cos=None, sin=None):
    T, D = h.shape
    tm, tn = _blk(T, 1024), _blk(n_cols, 1024)
    assert col0 % tn == 0
    c0 = col0 // tn
    in_specs = [pl.BlockSpec((tm, D), lambda j, i: (i, 0)),
                pl.BlockSpec((None, D, tn), lambda j, i: (layer, 0, c0 + j))]
    args = [h, w_in]
    if cos is None:
        body = _v_kernel
    else:
        hd = cos.shape[1]
        body = functools.partial(_qk_kernel, hd=hd)
        in_specs += [pl.BlockSpec((tm, hd), lambda j, i: (i, 0))] * 2
        args += [cos, sin]
    return pl.pallas_call(
        body,
        grid=(n_cols // tn, T // tm),
        in_specs=in_specs,
        out_specs=pl.BlockSpec((tm, tn), lambda j, i: (i, j)),
        out_shape=jax.ShapeDtypeStruct((T, n_cols), BF16),
        scratch_shapes=[pltpu.VMEM((D, tn), BF16)],
        compiler_params=_params("parallel", "arbitrary"),
        name="qk_proj" if cos is not None else "v_proj",
    )(*args)


def _attn_kernel(scal_ref, lam_ref, subln_ref, q_ref, k_ref, v_ref, o_ref,
                 vt_ref, qt_ref, sa_ref, sb_ref, m1_ref, l1_ref, a1_ref, m2_ref, l2_ref, a2_ref,
                 *, blk, hd):
    nq = vt_ref.shape[0]
    unroll = 4 if nq % 4 == 0 else nq
    c = hd ** -0.5 * math.log2(math.e)

    for n in range(nq):
        rows = slice(n * blk, (n + 1) * blk)
        vt_ref[n] = v_ref[rows, :].T
        qt_ref[n, 0] = q_ref[rows, :hd].T
        qt_ref[n, 1] = q_ref[rows, hd:].T

    lp = lam_ref[...]
    lam_init = scal_ref[0]
    lam_full = (jnp.exp(jnp.sum(lp[0:1] * lp[1:2], keepdims=True))
                - jnp.exp(jnp.sum(lp[2:3] * lp[3:4], keepdims=True)) + lam_init)

    def update(s, vt, m_ref, l_ref, a_ref):
        m_old = m_ref[...]
        m_new = jnp.maximum(m_old, jnp.max(s, axis=0, keepdims=True))
        alpha = jnp.exp2((m_old - m_new) * c)
        p = jnp.exp2((s - m_new) * c)
        l_ref[...] = alpha * l_ref[...] + jnp.sum(p, axis=0, keepdims=True)
        a_ref[...] = alpha * a_ref[...] + _mm(vt, p.astype(vt.dtype))
        m_ref[...] = m_new

    def init_stats():
        for m_ref, l_ref, a_ref in ((m1_ref, l1_ref, a1_ref), (m2_ref, l2_ref, a2_ref)):
            m_ref[...] = jnp.full(m_ref.shape, -jnp.inf, F32)
            l_ref[...] = jnp.zeros(l_ref.shape, F32)
            a_ref[...] = jnp.zeros(a_ref.shape, F32)

    def scores(qi, j, buf):
        k = k_ref[pl.ds(pl.multiple_of(j * blk, blk), blk), :]
        buf[0] = _mm(k[:, :hd], qt_ref[qi, 0])
        buf[1] = _mm(k[:, hd:], qt_ref[qi, 1])

    def process(j, buf, masked):
        vt = vt_ref[j]
        s1 = buf[0]
        s2 = buf[1]
        if masked:
            r = lax.broadcasted_iota(jnp.int32, s1.shape, 0)
            q = lax.broadcasted_iota(jnp.int32, s1.shape, 1)
            s1 = jnp.where(r <= q, s1, -jnp.inf)
            s2 = jnp.where(r <= q, s2, -jnp.inf)
        update(s1, vt, m1_ref, l1_ref, a1_ref)
        update(s2, vt, m2_ref, l2_ref, a2_ref)

    def finish(qi):
        o = a1_ref[...] / l1_ref[...] - lam_full * (a2_ref[...] / l2_ref[...])
        init_stats()
        y = o * lax.rsqrt(jnp.mean(o * o, axis=0, keepdims=True) + EPS)
        y = (y * subln_ref[...]) * (1.0 - lam_init)
        o_ref[pl.ds(pl.multiple_of(qi * blk, blk), blk), :] = y.T.astype(o_ref.dtype)

    def q_block(qi, odd, cur, oth):
        def body(jj, carry):
            j = 2 * jj
            scores(qi, j + 1, oth)
            process(j, cur, masked=False)
            scores(qi, j + 2, cur)
            process(j + 1, oth, masked=False)
            return carry

        lax.fori_loop(0, qi // 2, body, 0)
        nxt = jnp.minimum(qi + 1, nq - 1)
        if odd:
            scores(qi, qi, oth)
            process(qi - 1, cur, masked=False)
            scores(nxt, 0, cur)
            process(qi, oth, masked=True)
            first = cur
        else:
            scores(nxt, 0, oth)
            process(qi, cur, masked=True)
            first = oth
        finish(qi)
        return first

    def q_group(g, carry):
        cur = sa_ref
        for r in range(unroll):
            oth = sb_ref if cur is sa_ref else sa_ref
            cur = q_block(g * unroll + r, r % 2 == 1, cur, oth)
        assert cur is sa_ref or nq == unroll
        return carry

    init_stats()
    scores(0, 0, sa_ref)
    if nq == unroll:
        q_group(0, 0)
    else:
        lax.fori_loop(0, nq // unroll, q_group, 0)


def _diff_attention(qk, v, lam_l, subln_l, lam_init, B, S, H, hd, blk_pref=512):
    vhd = 2 * hd
    blk = _blk(S, blk_pref)
    qk3 = qk.reshape(B, S, 2 * H * vhd)
    v3 = v.reshape(B, S, H * vhd)
    scal = jnp.full((1,), lam_init, F32)
    stat = pltpu.VMEM((1, blk), F32)
    acc = pltpu.VMEM((vhd, blk), F32)
    scores = pltpu.VMEM((2, blk, blk), F32)
    nq = S // blk
    return pl.pallas_call(
        functools.partial(_attn_kernel, blk=blk, hd=hd),
        grid=(B, H),
        in_specs=[pl.BlockSpec(memory_space=pltpu.SMEM),
                  pl.BlockSpec((4, hd), lambda b, h: (0, 0)),
                  pl.BlockSpec((vhd, 1), lambda b, h: (0, 0)),
                  pl.BlockSpec((None, S, vhd), lambda b, h: (b, 0, h)),
                  pl.BlockSpec((None, S, vhd), lambda b, h: (b, 0, H + h)),
                  pl.BlockSpec((None, S, vhd), lambda b, h: (b, 0, h))],
        out_specs=pl.BlockSpec((None, S, vhd), lambda b, h: (b, 0, h)),
        out_shape=jax.ShapeDtypeStruct((B, S, H * vhd), BF16),
        scratch_shapes=[pltpu.VMEM((nq, vhd, blk), BF16), pltpu.VMEM((nq, 2, hd, blk), BF16),
                        scores, scores, stat, stat, acc, stat, stat, acc],
        compiler_params=_params("parallel", "parallel"),
        name="diff_attention",
    )(scal, lam_l, subln_l.reshape(vhd, 1), qk3, qk3, v3).reshape(B * S, H * vhd)


def _cast_rows(w_ref, wb_ref, n_cast):
    i = pl.program_id(0)

    @pl.when(i < n_cast)
    def _():
        rc = w_ref.shape[0]
        wb_ref[pl.ds(pl.multiple_of(i * rc, rc), rc), :] = w_ref[...].astype(wb_ref.dtype)


def _sgu_kernel(h_ref, wu_ref, wv_ref, lng_ref, lnb_ref, ws_ref, bs_ref, o_ref,
                wub_ref, wvb_ref, u_ref, *, chunk, groups, n_cast):
    _cast_rows(wu_ref, wub_ref, n_cast)
    _cast_rows(wv_ref, wvb_ref, n_cast)

    @pl.when(pl.program_id(0) >= n_cast)
    def _():
        W = o_ref.shape[1]
        gd = W // groups
        cw = _blk(W, 512)
        h = h_ref[...]

        def gelu_cols(wb_ref, c0):
            z = _mm(h, wb_ref[:, c0:c0 + cw])
            return 0.5 * z * (1.0 + lax.erf(z * (2.0 ** -0.5)))

        vg = jnp.concatenate([gelu_cols(wvb_ref, c0) for c0 in range(0, W, cw)], axis=1)
        mu = jnp.mean(vg, axis=-1, keepdims=True)
        d = vg - mu
        var = jnp.mean(d * d, axis=-1, keepdims=True)
        vn = ((d * lax.rsqrt(var + EPS)) * lng_ref[...] + lnb_ref[...]).astype(BF16)
        for c0 in range(0, W, cw):
            u_ref[:, c0:c0 + cw] = gelu_cols(wub_ref, c0)
        r = lax.broadcasted_iota(jnp.int32, (chunk, chunk), 0)
        c = lax.broadcasted_iota(jnp.int32, (chunk, chunk), 1)
        causal = r >= c
        for g in range(groups):
            w = jnp.where(causal, ws_ref[g], 0.0).astype(BF16)
            bias = bs_ref[:, g:g + 1]
            cols = slice(g * gd, (g + 1) * gd)
            for n in range(o_ref.shape[0] // chunk):
                rows = slice(n * chunk, (n + 1) * chunk)
                mixed = _mm(w, vn[rows, cols]) + bias
                o_ref[rows, cols] = (u_ref[rows, cols] * mixed).astype(o_ref.dtype)


def _sgu(h, w_in, layer, col0, ln_g, ln_b, ws, bs_t, chunk, groups):
    T, D = h.shape
    W = ln_g.shape[0]
    assert col0 % W == 0
    tm = 2 * chunk if T % (2 * chunk) == 0 else chunk
    rc = _blk(D, 256)
    n_cast = D // rc
    row = lambda i: (jnp.maximum(i - n_cast, 0), 0)
    wrow = lambda half: (lambda i: (layer, jnp.minimum(i, n_cast - 1), col0 // W + half))
    return pl.pallas_call(
        functools.partial(_sgu_kernel, chunk=chunk, groups=groups, n_cast=n_cast),
        grid=(n_cast + T // tm,),
        in_specs=[pl.BlockSpec((tm, D), row),
                  pl.BlockSpec((None, rc, W), wrow(0)),
                  pl.BlockSpec((None, rc, W), wrow(1)),
                  pl.BlockSpec((1, W), lambda i: (0, 0)),
                  pl.BlockSpec((1, W), lambda i: (0, 0)),
                  pl.BlockSpec((groups, chunk, chunk), lambda i: (0, 0, 0)),
                  pl.BlockSpec((chunk, groups), lambda i: (0, 0))],
        out_specs=pl.BlockSpec((tm, W), row),
        out_shape=jax.ShapeDtypeStruct((T, W), BF16),
        scratch_shapes=[pltpu.VMEM((D, W), BF16), pltpu.VMEM((D, W), BF16),
                        pltpu.VMEM((tm, W), F32)],
        compiler_params=_params("arbitrary"),
        name="sgu",
    )(h, w_in, w_in, ln_g.reshape(1, W), ln_b.reshape(1, W), ws, bs_t)


def _merge_kernel(h_ref, o_ref, sg_ref, wga_ref, wgs_ref, wa_ref, ws_ref, ba_ref, bs_ref, out_ref,
                  wgab_ref, wgsb_ref, wab_ref, wsb_ref):
    for w_ref, wb_ref in ((wga_ref, wgab_ref), (wgs_ref, wgsb_ref), (wa_ref, wab_ref),
                          (ws_ref, wsb_ref)):
        _cache_bf16(w_ref, wb_ref)
    h = h_ref[...]
    g_attn = jax.nn.sigmoid(_mm(h, wgab_ref[...]) + ba_ref[...])
    g_sgu = jax.nn.sigmoid(_mm(h, wgsb_ref[...]) + bs_ref[...])
    y = g_attn * _mm(o_ref[...], wab_ref[...]) + g_sgu * _mm(sg_ref[...], wsb_ref[...])
    out_ref[...] = y.astype(out_ref.dtype)


def _merge(h, o, sg, w_in, layer, gate_col0, gate_b, w_branch):
    T, D = h.shape
    tm, tn = _blk(T, 1024), _blk(D, 256)
    nb = D // tn
    assert gate_col0 % tn == 0
    g0 = gate_col0 // tn
    row = lambda j, i: (i, 0)
    return pl.pallas_call(
        _merge_kernel,
        grid=(nb, T // tm),
        in_specs=[pl.BlockSpec((tm, D), row),
                  pl.BlockSpec((tm, o.shape[1]), row),
                  pl.BlockSpec((tm, sg.shape[1]), row),
                  pl.BlockSpec((None, D, tn), lambda j, i: (layer, 0, g0 + j)),
                  pl.BlockSpec((None, D, tn), lambda j, i: (layer, 0, g0 + nb + j)),
                  pl.BlockSpec((None, None, o.shape[1], tn), lambda j, i: (layer, 0, 0, j)),
                  pl.BlockSpec((None, None, sg.shape[1], tn), lambda j, i: (layer, 1, 0, j)),
                  pl.BlockSpec((1, tn), lambda j, i: (0, j)),
                  pl.BlockSpec((1, tn), lambda j, i: (0, nb + j))],
        out_specs=pl.BlockSpec((tm, tn), lambda j, i: (i, j)),
        out_shape=jax.ShapeDtypeStruct((T, D), BF16),
        scratch_shapes=[pltpu.VMEM((D, tn), BF16), pltpu.VMEM((D, tn), BF16),
                        pltpu.VMEM((o.shape[1], tn), BF16), pltpu.VMEM((sg.shape[1], tn), BF16)],
        compiler_params=_params("parallel", "arbitrary"),
        name="merge",
    )(h, o, sg, w_in, w_in, w_branch, w_branch, gate_b.reshape(1, -1), gate_b.reshape(1, -1))


def _proj_res_norm_kernel(a_ref, w_ref, x_ref, g_ref, *refs, n_cast):
    *out_refs, wb_ref = refs
    _cast_rows(w_ref, wb_ref, n_cast)

    @pl.when(pl.program_id(0) >= n_cast)
    def _():
        y = x_ref[...] + _mm(a_ref[...], wb_ref[...])
        if len(out_refs) == 2:
            out_refs[0][...] = y
        n = y * lax.rsqrt(jnp.mean(y * y, axis=-1, keepdims=True) + EPS)
        out_refs[-1][...] = (n * g_ref[...]).astype(out_refs[-1].dtype)


def _proj_res_norm(a, w_all, layer, x, g, norm_dtype, emit_x=True):
    T, K = a.shape
    D = w_all.shape[2]
    tm = _blk(T, 512 if K <= D else 256)
    rc = _blk(K, 512)
    n_cast = K // rc
    row = lambda i: (jnp.maximum(i - n_cast, 0), 0)
    out_specs = [pl.BlockSpec((tm, D), row)] * (2 if emit_x else 1)
    out_shape = ([jax.ShapeDtypeStruct((T, D), F32)] if emit_x else []) + [
        jax.ShapeDtypeStruct((T, D), norm_dtype)]
    outs = pl.pallas_call(
        functools.partial(_proj_res_norm_kernel, n_cast=n_cast),
        grid=(n_cast + T // tm,),
        in_specs=[pl.BlockSpec((tm, K), row),
                  pl.BlockSpec((None, rc, D), lambda i: (layer, jnp.minimum(i, n_cast - 1), 0)),
                  pl.BlockSpec((tm, D), row),
                  pl.BlockSpec((1, D), lambda i: (0, 0))],
        out_specs=out_specs,
        out_shape=out_shape,
        scratch_shapes=[pltpu.VMEM((K, D), BF16)],
        compiler_params=_params("arbitrary"),
        name="proj_res_norm",
    )(a, w_all, x, g.reshape(1, D))
    return outs if emit_x else outs[0]


def _ffn_in_kernel(h_ref, wa_ref, wb_ref, o_ref, wab_ref, wbb_ref):
    _cache_bf16(wa_ref, wab_ref)
    _cache_bf16(wb_ref, wbb_ref)
    h = h_ref[...]
    a = _mm(h, wab_ref[...])
    b = _mm(h, wbb_ref[...])
    o_ref[...] = (a * jax.nn.sigmoid(a) * b).astype(o_ref.dtype)


def _ffn_in(h, w_ffn_in, layer):
    T, D = h.shape
    F = w_ffn_in.shape[2] // 2
    tm, tn = _blk(T, 1024), _blk(F, 512)
    nb = F // tn
    return pl.pallas_call(
        _ffn_in_kernel,
        grid=(nb, T // tm),
        in_specs=[pl.BlockSpec((tm, D), lambda j, i: (i, 0)),
                  pl.BlockSpec((None, D, tn), lambda j, i: (layer, 0, j)),
                  pl.BlockSpec((None, D, tn), lambda j, i: (layer, 0, nb + j))],
        out_specs=pl.BlockSpec((tm, tn), lambda j, i: (i, j)),
        out_shape=jax.ShapeDtypeStruct((T, F), BF16),
        scratch_shapes=[pltpu.VMEM((D, tn), BF16)] * 2,
        compiler_params=_params("parallel", "arbitrary"),
        name="ffn_in",
    )(h, w_ffn_in, w_ffn_in)


def kernel(x, positions, norm_attn, w_in, gate_b, lam, subln, sgu_ln_g, sgu_ln_b, w_spatial,
           b_spatial, w_branch, w_out, norm_ffn, w_ffn_in, w_ffn_out, norm_final):
    B, S, D = x.shape
    T = B * S
    depth = w_in.shape[0]
    hd = lam.shape[-1]
    H = D // subln.shape[-1]
    groups, chunk = w_spatial.shape[1], w_spatial.shape[2]
    W = sgu_ln_g.shape[-1]
    assert S % chunk == 0 and w_in.shape[2] == 3 * D + 2 * W + 2 * D

    inv_freq = ROPE_THETA ** (-jnp.arange(0, hd, 2, dtype=F32) / hd)
    invf = jnp.concatenate([inv_freq, inv_freq]).reshape(1, hd)
    cos, sin = _rope_tables(positions.reshape(T, 1), invf)

    xf = x.reshape(T, D)
    h = _rmsnorm(xf, norm_attn[0], BF16)
    for l in range(depth):
        lam_init = 0.8 - 0.6 * math.exp(-0.3 * l)
        qk = _in_proj(h, w_in, l, 0, 2 * D, cos, sin)
        v = _in_proj(h, w_in, l, 2 * D, D)
        o = _diff_attention(qk, v, lam[l], subln[l], lam_init, B, S, H, hd)
        sg = _sgu(h, w_in, l, 3 * D, sgu_ln_g[l], sgu_ln_b[l], w_spatial[l], b_spatial[l].T,
                  chunk, groups)
        merged = _merge(h, o, sg, w_in, l, 3 * D + 2 * W, gate_b[l], w_branch)
        xf, h2 = _proj_res_norm(merged, w_out, l, xf, norm_ffn[l], BF16)

        act = _ffn_in(h2, w_ffn_in, l)
        if l + 1 < depth:
            xf, h = _proj_res_norm(act, w_ffn_out, l, xf, norm_attn[l + 1], BF16)
        else:
            out = _proj_res_norm(act, w_ffn_out, l, xf, norm_final, x.dtype, emit_x=False)
    return out.reshape(B, S, D)
```

```python
import functools
import math

import jax
import jax.numpy as jnp
from jax import lax
from jax.experimental import pallas as pl
from jax.experimental.pallas import tpu as pltpu

ROPE_THETA = 10000.0
EPS = 1e-6
LANES = 128
MXU_COLS = 256
VMEM_LIMIT_BYTES = 56 * 1024 * 1024

F32 = jnp.float32
BF16 = jnp.bfloat16


def _blk(dim, pref):
    if dim <= pref:
        return dim
    b = pref - pref % LANES
    while b > LANES and dim % b:
        b -= LANES
    assert dim % b == 0, (dim, pref)
    return b


def _params(*sem):
    return pltpu.CompilerParams(dimension_semantics=sem, vmem_limit_bytes=VMEM_LIMIT_BYTES)


def _mm(a, b):
    return jnp.dot(a, b, preferred_element_type=F32)


def _rope_kernel(pos_ref, invf_ref, cos_ref, sin_ref):
    ang = pos_ref[...].astype(F32) * invf_ref[...]
    lane = lax.broadcasted_iota(jnp.int32, ang.shape, 1)
    sin = jnp.sin(ang)
    cos_ref[...] = jnp.cos(ang)
    sin_ref[...] = jnp.where(lane < ang.shape[1] // 2, -sin, sin)


def _rope_tables(pos, invf):
    T = pos.shape[0]
    hd = invf.shape[1]
    tm = _blk(T, 2048)
    return pl.pallas_call(
        _rope_kernel,
        grid=(T // tm,),
        in_specs=[pl.BlockSpec((tm, 1), lambda i: (i, 0)),
                  pl.BlockSpec((1, hd), lambda i: (0, 0))],
        out_specs=[pl.BlockSpec((tm, hd), lambda i: (i, 0)),
                   pl.BlockSpec((tm, hd), lambda i: (i, 0))],
        out_shape=[jax.ShapeDtypeStruct((T, hd), F32)] * 2,
        compiler_params=_params("parallel"),
        name="rope_tables",
    )(pos, invf)


def _rmsnorm_kernel(x_ref, g_ref, o_ref):
    x = x_ref[...]
    y = x * lax.rsqrt(jnp.mean(x * x, axis=-1, keepdims=True) + EPS)
    o_ref[...] = (y * g_ref[...]).astype(o_ref.dtype)


def _rmsnorm(x, g, out_dtype):
    T, D = x.shape
    tm = _blk(T, 512)
    return pl.pallas_call(
        _rmsnorm_kernel,
        grid=(T // tm,),
        in_specs=[pl.BlockSpec((tm, D), lambda i: (i, 0)),
                  pl.BlockSpec((1, D), lambda i: (0, 0))],
        out_specs=pl.BlockSpec((tm, D), lambda i: (i, 0)),
        out_shape=jax.ShapeDtypeStruct((T, D), out_dtype),
        compiler_params=_params("parallel"),
        name="rmsnorm",
    )(x, g.reshape(1, D))


def _cache_bf16(w_ref, wb_ref):
    @pl.when(pl.program_id(1) == 0)
    def _():
        wb_ref[...] = w_ref[...].astype(wb_ref.dtype)


def _qk_kernel(h_ref, w_ref, cos_ref, sin_ref, o_ref, wb_ref, *, hd):
    _cache_bf16(w_ref, wb_ref)
    h = h_ref[...]
    cos = cos_ref[...]
    sin = sin_ref[...]
    for c0 in range(0, o_ref.shape[1], MXU_COLS):
        acc = _mm(h, wb_ref[:, c0:c0 + MXU_COLS])
        for s0 in range(0, MXU_COLS, hd):
            t = acc[:, s0:s0 + hd]
            r = t * cos + pltpu.roll(t, hd // 2, 1) * sin
            o_ref[:, c0 + s0:c0 + s0 + hd] = r.astype(o_ref.dtype)


def _v_kernel(h_ref, w_ref, o_ref, wb_ref):
    _cache_bf16(w_ref, wb_ref)
    h = h_ref[...]
    for c0 in range(0, o_ref.shape[1], MXU_COLS):
        o_ref[:, c0:c0 + MXU_COLS] = _mm(h, wb_ref[:, c0:c0 + MXU_COLS]).astype(o_ref.dtype)


def _in_proj(h, w_in, layer, col0, n_cols, cos=None, sin=None):
    T, D = h.shape
    tm, tn = _blk(T, 1024), _blk(n_cols, 1024)
    assert col0 % tn == 0
    c0 = col0 // tn
    in_specs = [pl.BlockSpec((tm, D), lambda j, i: (i, 0)),
                pl.BlockSpec((None, D, tn), lambda j, i: (layer, 0, c0 + j))]
    args = [h, w_in]
    if cos is None:
        body = _v_kernel
    else:
        hd = cos.shape[1]
        body = functools.partial(_qk_kernel, hd=hd)
        in_specs += [pl.BlockSpec((tm, hd), lambda j, i: (i, 0))] * 2
        args += [cos, sin]
    return pl.pallas_call(
        body,
        grid=(n_cols // tn, T // tm),
        in_specs=in_specs,
        out_specs=pl.BlockSpec((tm, tn), lambda j, i: (i, j)),
        out_shape=jax.ShapeDtypeStruct((T, n_cols), BF16),
        scratch_shapes=[pltpu.VMEM((D, tn), BF16)],
        compiler_params=_params("parallel", "arbitrary"),
        name="qk_proj" if cos is not None else "v_proj",
    )(*args)


def _attn_kernel(scal_ref, lam_ref, subln_ref, q_ref, k_ref, v_ref, o_ref,
                 vt_ref, qt_ref, sa_ref, sb_ref, xa_ref, xb_ref,
                 m1_ref, l1_ref, a1_ref, m2_ref, l2_ref, a2_ref,
                 *, blk, hd):
    nq = vt_ref.shape[0]
    unroll = 4 if nq % 4 == 0 else nq
    c = hd ** -0.5 * math.log2(math.e)

    for n in range(nq):
        rows = slice(n * blk, (n + 1) * blk)
        vt_ref[n] = v_ref[rows, :].T
        qt_ref[n, 0] = q_ref[rows, :hd].T
        qt_ref[n, 1] = q_ref[rows, hd:].T

    lp = lam_ref[...]
    lam_init = scal_ref[0]
    lam_full = (jnp.exp(jnp.sum(lp[0:1] * lp[1:2], keepdims=True))
                - jnp.exp(jnp.sum(lp[2:3] * lp[3:4], keepdims=True)) + lam_init)

    def update(s, smax, vt, m_ref, l_ref, a_ref):
        m_old = m_ref[...]
        m_new = jnp.maximum(m_old, smax)
        alpha = jnp.exp2((m_old - m_new) * c)
        p = jnp.exp2((s - m_new) * c)
        l_ref[...] = alpha * l_ref[...] + jnp.sum(p, axis=0, keepdims=True)
        a_ref[...] = alpha * a_ref[...] + _mm(vt, p.astype(vt.dtype))
        m_ref[...] = m_new

    def init_stats():
        for m_ref, l_ref, a_ref in ((m1_ref, l1_ref, a1_ref), (m2_ref, l2_ref, a2_ref)):
            m_ref[...] = jnp.full(m_ref.shape, -jnp.inf, F32)
            l_ref[...] = jnp.zeros(l_ref.shape, F32)
            a_ref[...] = jnp.zeros(a_ref.shape, F32)

    def scores(qi, j, slot):
        buf, mx = slot
        k = k_ref[pl.ds(pl.multiple_of(j * blk, blk), blk), :]
        for i in range(2):
            s = _mm(k[:, i * hd:(i + 1) * hd], qt_ref[qi, i])
            buf[i] = s
            mx[i] = jnp.max(s, axis=0, keepdims=True)

    def process(j, slot, masked):
        buf, mx = slot
        vt = vt_ref[j]
        for i, (m_ref, l_ref, a_ref) in enumerate(((m1_ref, l1_ref, a1_ref),
                                                   (m2_ref, l2_ref, a2_ref))):
            s = buf[i]
            if masked:
                r = lax.broadcasted_iota(jnp.int32, s.shape, 0)
                q = lax.broadcasted_iota(jnp.int32, s.shape, 1)
                s = jnp.where(r <= q, s, -jnp.inf)
                smax = jnp.max(s, axis=0, keepdims=True)
            else:
                smax = mx[i]
            update(s, smax, vt, m_ref, l_ref, a_ref)

    def finish(qi):
        o = a1_ref[...] / l1_ref[...] - lam_full * (a2_ref[...] / l2_ref[...])
        init_stats()
        y = o * lax.rsqrt(jnp.mean(o * o, axis=0, keepdims=True) + EPS)
        y = (y * subln_ref[...]) * (1.0 - lam_init)
        o_ref[pl.ds(pl.multiple_of(qi * blk, blk), blk), :] = y.T.astype(o_ref.dtype)

    def q_block(qi, odd, cur, oth):
        def body(jj, carry):
            j = 2 * jj
            scores(qi, j + 1, oth)
            process(j, cur, masked=False)
            scores(qi, j + 2, cur)
            process(j + 1, oth, masked=False)
            return carry

        lax.fori_loop(0, qi // 2, body, 0)
        nxt = jnp.minimum(qi + 1, nq - 1)
        if odd:
            scores(qi, qi, oth)
            process(qi - 1, cur, masked=False)
            scores(nxt, 0, cur)
            process(qi, oth, masked=True)
            first = cur
        else:
            scores(nxt, 0, oth)
            process(qi, cur, masked=True)
            first = oth
        finish(qi)
        return first

    slot_a, slot_b = (sa_ref, xa_ref), (sb_ref, xb_ref)

    def q_group(g, carry):
        cur = slot_a
        for r in range(unroll):
            oth = slot_b if cur is slot_a else slot_a
            cur = q_block(g * unroll + r, r % 2 == 1, cur, oth)
        assert cur is slot_a or nq == unroll
        return carry

    init_stats()
    scores(0, 0, slot_a)
    if nq == unroll:
        q_group(0, 0)
    else:
        lax.fori_loop(0, nq // unroll, q_group, 0)


def _diff_attention(qk, v, lam_l, subln_l, lam_init, B, S, H, hd, blk_pref=512):
    vhd = 2 * hd
    blk = _blk(S, blk_pref)
    qk3 = qk.reshape(B, S, 2 * H * vhd)
    v3 = v.reshape(B, S, H * vhd)
    scal = jnp.full((1,), lam_init, F32)
    stat = pltpu.VMEM((1, blk), F32)
    acc = pltpu.VMEM((vhd, blk), F32)
    scores = pltpu.VMEM((2, blk, blk), F32)
    nq = S // blk
    return pl.pallas_call(
        functools.partial(_attn_kernel, blk=blk, hd=hd),
        grid=(B, H),
        in_specs=[pl.BlockSpec(memory_space=pltpu.SMEM),
                  pl.BlockSpec((4, hd), lambda b, h: (0, 0)),
                  pl.BlockSpec((vhd, 1), lambda b, h: (0, 0)),
                  pl.BlockSpec((None, S, vhd), lambda b, h: (b, 0, h)),
                  pl.BlockSpec((None, S, vhd), lambda b, h: (b, 0, H + h)),
                  pl.BlockSpec((None, S, vhd), lambda b, h: (b, 0, h))],
        out_specs=pl.BlockSpec((None, S, vhd), lambda b, h: (b, 0, h)),
        out_shape=jax.ShapeDtypeStruct((B, S, H * vhd), BF16),
        scratch_shapes=[pltpu.VMEM((nq, vhd, blk), BF16), pltpu.VMEM((nq, 2, hd, blk), BF16),
                        scores, scores, pltpu.VMEM((2, 1, blk), F32), pltpu.VMEM((2, 1, blk), F32),
                        stat, stat, acc, stat, stat, acc],
        compiler_params=_params("parallel", "parallel"),
        name="diff_attention",
    )(scal, lam_l, subln_l.reshape(vhd, 1), qk3, qk3, v3).reshape(B * S, H * vhd)


def _cast_rows(w_ref, wb_ref, n_cast):
    i = pl.program_id(0)

    @pl.when(i < n_cast)
    def _():
        rc = w_ref.shape[0]
        wb_ref[pl.ds(pl.multiple_of(i * rc, rc), rc), :] = w_ref[...].astype(wb_ref.dtype)


def _sgu_kernel(h_ref, wu_ref, wv_ref, lng_ref, lnb_ref, ws_ref, bs_ref, o_ref,
                wub_ref, wvb_ref, u_ref, *, chunk, groups, n_cast):
    _cast_rows(wu_ref, wub_ref, n_cast)
    _cast_rows(wv_ref, wvb_ref, n_cast)

    @pl.when(pl.program_id(0) >= n_cast)
    def _():
        W = o_ref.shape[1]
        gd = W // groups
        cw = _blk(W, 512)
        h = h_ref[...]

        def gelu_cols(wb_ref, c0):
            z = _mm(h, wb_ref[:, c0:c0 + cw])
            return 0.5 * z * (1.0 + lax.erf(z * (2.0 ** -0.5)))

        vg = jnp.concatenate([gelu_cols(wvb_ref, c0) for c0 in range(0, W, cw)], axis=1)
        mu = jnp.mean(vg, axis=-1, keepdims=True)
        d = vg - mu
        var = jnp.mean(d * d, axis=-1, keepdims=True)
        vn = ((d * lax.rsqrt(var + EPS)) * lng_ref[...] + lnb_ref[...]).astype(BF16)
        for c0 in range(0, W, cw):
            u_ref[:, c0:c0 + cw] = gelu_cols(wub_ref, c0)
        r = lax.broadcasted_iota(jnp.int32, (chunk, chunk), 0)
        c = lax.broadcasted_iota(jnp.int32, (chunk, chunk), 1)
        causal = r >= c
        for g in range(groups):
            w = jnp.where(causal, ws_ref[g], 0.0).astype(BF16)
            bias = bs_ref[:, g:g + 1]
            cols = slice(g * gd, (g + 1) * gd)
            for n in range(o_ref.shape[0] // chunk):
                rows = slice(n * chunk, (n + 1) * chunk)
                mixed = _mm(w, vn[rows, cols]) + bias
                o_ref[rows, cols] = (u_ref[rows, cols] * mixed).astype(o_ref.dtype)


def _sgu(h, w_in, layer, col0, ln_g, ln_b, ws, bs_t, chunk, groups):
    T, D = h.shape
    W = ln_g.shape[0]
    assert col0 % W == 0
    tm = 2 * chunk if T % (2 * chunk) == 0 else chunk
    rc = _blk(D, 256)
    n_cast = D // rc
    row = lambda i: (jnp.maximum(i - n_cast, 0), 0)
    wrow = lambda half: (lambda i: (layer, jnp.minimum(i, n_cast - 1), col0 // W + half))
    return pl.pallas_call(
        functools.partial(_sgu_kernel, chunk=chunk, groups=groups, n_cast=n_cast),
        grid=(n_cast + T // tm,),
        in_specs=[pl.BlockSpec((tm, D), row),
                  pl.BlockSpec((None, rc, W), wrow(0)),
                  pl.BlockSpec((None, rc, W), wrow(1)),
                  pl.BlockSpec((1, W), lambda i: (0, 0)),
                  pl.BlockSpec((1, W), lambda i: (0, 0)),
                  pl.BlockSpec((groups, chunk, chunk), lambda i: (0, 0, 0)),
                  pl.BlockSpec((chunk, groups), lambda i: (0, 0))],
        out_specs=pl.BlockSpec((tm, W), row),
        out_shape=jax.ShapeDtypeStruct((T, W), BF16),
        scratch_shapes=[pltpu.VMEM((D, W), BF16), pltpu.VMEM((D, W), BF16),
                        pltpu.VMEM((tm, W), F32)],
        compiler_params=_params("arbitrary"),
        name="sgu",
    )(h, w_in, w_in, ln_g.reshape(1, W), ln_b.reshape(1, W), ws, bs_t)


def _merge_kernel(h_ref, o_ref, sg_ref, wga_ref, wgs_ref, wa_ref, ws_ref, ba_ref, bs_ref, out_ref,
                  wgab_ref, wgsb_ref, wab_ref, wsb_ref):
    for w_ref, wb_ref in ((wga_ref, wgab_ref), (wgs_ref, wgsb_ref), (wa_ref, wab_ref),
                          (ws_ref, wsb_ref)):
        _cache_bf16(w_ref, wb_ref)
    h = h_ref[...]
    g_attn = jax.nn.sigmoid(_mm(h, wgab_ref[...]) + ba_ref[...])
    g_sgu = jax.nn.sigmoid(_mm(h, wgsb_ref[...]) + bs_ref[...])
    y = g_attn * _mm(o_ref[...], wab_ref[...]) + g_sgu * _mm(sg_ref[...], wsb_ref[...])
    out_ref[...] = y.astype(out_ref.dtype)


def _merge(h, o, sg, w_in, layer, gate_col0, gate_b, w_branch):
    T, D = h.shape
    tm, tn = _blk(T, 1024), _blk(D, 256)
    nb = D // tn
    assert gate_col0 % tn == 0
    g0 = gate_col0 // tn
    row = lambda j, i: (i, 0)
    return pl.pallas_call(
        _merge_kernel,
        grid=(nb, T // tm),
        in_specs=[pl.BlockSpec((tm, D), row),
                  pl.BlockSpec((tm, o.shape[1]), row),
                  pl.BlockSpec((tm, sg.shape[1]), row),
                  pl.BlockSpec((None, D, tn), lambda j, i: (layer, 0, g0 + j)),
                  pl.BlockSpec((None, D, tn), lambda j, i: (layer, 0, g0 + nb + j)),
                  pl.BlockSpec((None, None, o.shape[1], tn), lambda j, i: (layer, 0, 0, j)),
                  pl.BlockSpec((None, None, sg.shape[1], tn), lambda j, i: (layer, 1, 0, j)),
                  pl.BlockSpec((1, tn), lambda j, i: (0, j)),
                  pl.BlockSpec((1, tn), lambda j, i: (0, nb + j))],
        out_specs=pl.BlockSpec((tm, tn), lambda j, i: (i, j)),
        out_shape=jax.ShapeDtypeStruct((T, D), BF16),
        scratch_shapes=[pltpu.VMEM((D, tn), BF16), pltpu.VMEM((D, tn), BF16),
                        pltpu.VMEM((o.shape[1], tn), BF16), pltpu.VMEM((sg.shape[1], tn), BF16)],
        compiler_params=_params("parallel", "arbitrary"),
        name="merge",
    )(h, o, sg, w_in, w_in, w_branch, w_branch, gate_b.reshape(1, -1), gate_b.reshape(1, -1))


def _proj_res_norm_kernel(a_ref, w_ref, x_ref, g_ref, *refs, n_cast):
    *out_refs, wb_ref = refs
    _cast_rows(w_ref, wb_ref, n_cast)

    @pl.when(pl.program_id(0) >= n_cast)
    def _():
        y = x_ref[...] + _mm(a_ref[...], wb_ref[...])
        if len(out_refs) == 2:
            out_refs[0][...] = y
        n = y * lax.rsqrt(jnp.mean(y * y, axis=-1, keepdims=True) + EPS)
        out_refs[-1][...] = (n * g_ref[...]).astype(out_refs[-1].dtype)


def _proj_res_norm(a, w_all, layer, x, g, norm_dtype, emit_x=True):
    T, K = a.shape
    D = w_all.shape[2]
    tm = _blk(T, 512 if K <= D else 256)
    rc = _blk(K, 512)
    n_cast = K // rc
    row = lambda i: (jnp.maximum(i - n_cast, 0), 0)
    out_specs = [pl.BlockSpec((tm, D), row)] * (2 if emit_x else 1)
    out_shape = ([jax.ShapeDtypeStruct((T, D), F32)] if emit_x else []) + [
        jax.ShapeDtypeStruct((T, D), norm_dtype)]
    outs = pl.pallas_call(
        functools.partial(_proj_res_norm_kernel, n_cast=n_cast),
        grid=(n_cast + T // tm,),
        in_specs=[pl.BlockSpec((tm, K), row),
                  pl.BlockSpec((None, rc, D), lambda i: (layer, jnp.minimum(i, n_cast - 1), 0)),
                  pl.BlockSpec((tm, D), row),
                  pl.BlockSpec((1, D), lambda i: (0, 0))],
        out_specs=out_specs,
        out_shape=out_shape,
        scratch_shapes=[pltpu.VMEM((K, D), BF16)],
        compiler_params=_params("arbitrary"),
        name="proj_res_norm",
    )(a, w_all, x, g.reshape(1, D))
    return outs if emit_x else outs[0]


def _ffn_in_kernel(h_ref, wa_ref, wb_ref, o_ref, wab_ref, wbb_ref):
    _cache_bf16(wa_ref, wab_ref)
    _cache_bf16(wb_ref, wbb_ref)
    h = h_ref[...]
    a = _mm(h, wab_ref[...])
    b = _mm(h, wbb_ref[...])
    o_ref[...] = (a * jax.nn.sigmoid(a) * b).astype(o_ref.dtype)


def _ffn_in(h, w_ffn_in, layer):
    T, D = h.shape
    F = w_ffn_in.shape[2] // 2
    tm, tn = _blk(T, 1024), _blk(F, 512)
    nb = F // tn
    return pl.pallas_call(
        _ffn_in_kernel,
        grid=(nb, T // tm),
        in_specs=[pl.BlockSpec((tm, D), lambda j, i: (i, 0)),
                  pl.BlockSpec((None, D, tn), lambda j, i: (layer, 0, j)),
                  pl.BlockSpec((None, D, tn), lambda j, i: (layer, 0, nb + j))],
        out_specs=pl.BlockSpec((tm, tn), lambda j, i: (i, j)),
        out_shape=jax.ShapeDtypeStruct((T, F), BF16),
        scratch_shapes=[pltpu.VMEM((D, tn), BF16)] * 2,
        compiler_params=_params("parallel", "arbitrary"),
        name="ffn_in",
    )(h, w_ffn_in, w_ffn_in)


def kernel(x, positions, norm_attn, w_in, gate_b, lam, subln, sgu_ln_g, sgu_ln_b, w_spatial,
           b_spatial, w_branch, w_out, norm_ffn, w_ffn_in, w_ffn_out, norm_final):
    B, S, D = x.shape
    T = B * S
    depth = w_in.shape[0]
    hd = lam.shape[-1]
    H = D // subln.shape[-1]
    groups, chunk = w_spatial.shape[1], w_spatial.shape[2]
    W = sgu_ln_g.shape[-1]
    assert S % chunk == 0 and w_in.shape[2] == 3 * D + 2 * W + 2 * D

    inv_freq = ROPE_THETA ** (-jnp.arange(0, hd, 2, dtype=F32) / hd)
    invf = jnp.concatenate([inv_freq, inv_freq]).reshape(1, hd)
    cos, sin = _rope_tables(positions.reshape(T, 1), invf)

    xf = x.reshape(T, D)
    h = _rmsnorm(xf, norm_attn[0], BF16)
    for l in range(depth):
        lam_init = 0.8 - 0.6 * math.exp(-0.3 * l)
        qk = _in_proj(h, w_in, l, 0, 2 * D, cos, sin)
        v = _in_proj(h, w_in, l, 2 * D, D)
        o = _diff_attention(qk, v, lam[l], subln[l], lam_init, B, S, H, hd)
        sg = _sgu(h, w_in, l, 3 * D, sgu_ln_g[l], sgu_ln_b[l], w_spatial[l], b_spatial[l].T,
                  chunk, groups)
        merged = _merge(h, o, sg, w_in, l, 3 * D + 2 * W, gate_b[l], w_branch)
        xf, h2 = _proj_res_norm(merged, w_out, l, xf, norm_ffn[l], BF16)

        act = _ffn_in(h2, w_ffn_in, l)
        if l + 1 < depth:
            xf, h = _proj_res_norm(act, w_ffn_out, l, xf, norm_attn[l + 1], BF16)
        else:
            out = _proj_res_norm(act, w_ffn_out, l, xf, norm_final, x.dtype, emit_x=False)
    return out.reshape(B, S, D)
```

```python
import functools
import math

import jax
import jax.numpy as jnp
from jax import lax
from jax.experimental import pallas as pl
from jax.experimental.pallas import tpu as pltpu

ROPE_THETA = 10000.0
EPS = 1e-6
LANES = 128
MXU_COLS = 256
ROW_CHUNK = 256
VMEM_LIMIT_BYTES = 56 * 1024 * 1024

F32 = jnp.float32
BF16 = jnp.bfloat16


def _blk(dim, pref):
    if dim <= pref:
        return dim
    b = pref - pref % LANES
    while b > LANES and dim % b:
        b -= LANES
    assert dim % b == 0, (dim, pref)
    return b


def _params(*sem):
    return pltpu.CompilerParams(dimension_semantics=sem, vmem_limit_bytes=VMEM_LIMIT_BYTES)


def _mm(a, b):
    return jnp.dot(a, b, preferred_element_type=F32)


def _rope_kernel(pos_ref, invf_ref, cos_ref, sin_ref):
    ang = pos_ref[...].astype(F32) * invf_ref[...]
    lane = lax.broadcasted_iota(jnp.int32, ang.shape, 1)
    sin = jnp.sin(ang)
    cos_ref[...] = jnp.cos(ang)
    sin_ref[...] = jnp.where(lane < ang.shape[1] // 2, -sin, sin)


def _rope_tables(pos, invf):
    T = pos.shape[0]
    hd = invf.shape[1]
    tm = _blk(T, 2048)
    return pl.pallas_call(
        _rope_kernel,
        grid=(T // tm,),
        in_specs=[pl.BlockSpec((tm, 1), lambda i: (i, 0)),
                  pl.BlockSpec((1, hd), lambda i: (0, 0))],
        out_specs=[pl.BlockSpec((tm, hd), lambda i: (i, 0)),
                   pl.BlockSpec((tm, hd), lambda i: (i, 0))],
        out_shape=[jax.ShapeDtypeStruct((T, hd), F32)] * 2,
        compiler_params=_params("parallel"),
        name="rope_tables",
    )(pos, invf)


def _rmsnorm_kernel(x_ref, g_ref, o_ref):
    x = x_ref[...]
    y = x * lax.rsqrt(jnp.mean(x * x, axis=-1, keepdims=True) + EPS)
    o_ref[...] = (y * g_ref[...]).astype(o_ref.dtype)


def _rmsnorm(x, g, out_dtype):
    T, D = x.shape
    tm = _blk(T, 512)
    return pl.pallas_call(
        _rmsnorm_kernel,
        grid=(T // tm,),
        in_specs=[pl.BlockSpec((tm, D), lambda i: (i, 0)),
                  pl.BlockSpec((1, D), lambda i: (0, 0))],
        out_specs=pl.BlockSpec((tm, D), lambda i: (i, 0)),
        out_shape=jax.ShapeDtypeStruct((T, D), out_dtype),
        compiler_params=_params("parallel"),
        name="rmsnorm",
    )(x, g.reshape(1, D))


def _cache_bf16(w_ref, wb_ref):
    @pl.when(pl.program_id(1) == 0)
    def _():
        wb_ref[...] = w_ref[...].astype(wb_ref.dtype)


def _qk_kernel(h_ref, w_ref, cos_ref, sin_ref, o_ref, wb_ref, *, hd):
    _cache_bf16(w_ref, wb_ref)
    rc = _blk(o_ref.shape[0], ROW_CHUNK)
    for r0 in range(0, o_ref.shape[0], rc):
        rows = slice(r0, r0 + rc)
        acc = _mm(h_ref[rows, :], wb_ref[...])
        cos = cos_ref[rows, :]
        sin = sin_ref[rows, :]
        for s0 in range(0, o_ref.shape[1], hd):
            t = acc[:, s0:s0 + hd]
            r = t * cos + pltpu.roll(t, hd // 2, 1) * sin
            o_ref[rows, s0:s0 + hd] = r.astype(o_ref.dtype)


def _v_kernel(h_ref, w_ref, o_ref, wb_ref):
    _cache_bf16(w_ref, wb_ref)
    h = h_ref[...]
    for c0 in range(0, o_ref.shape[1], MXU_COLS):
        o_ref[:, c0:c0 + MXU_COLS] = _mm(h, wb_ref[:, c0:c0 + MXU_COLS]).astype(o_ref.dtype)


def _in_proj(h, w_in, layer, col0, n_cols, cos=None, sin=None):
    T, D = h.shape
    tm, tn = _blk(T, 1024), _blk(n_cols, 1024)
    assert col0 % tn == 0
    c0 = col0 // tn
    in_specs = [pl.BlockSpec((tm, D), lambda j, i: (i, 0)),
                pl.BlockSpec((None, D, tn), lambda j, i: (layer, 0, c0 + j))]
    args = [h, w_in]
    if cos is None:
        body = _v_kernel
    else:
        hd = cos.shape[1]
        body = functools.partial(_qk_kernel, hd=hd)
        in_specs += [pl.BlockSpec((tm, hd), lambda j, i: (i, 0))] * 2
        args += [cos, sin]
    return pl.pallas_call(
        body,
        grid=(n_cols // tn, T // tm),
        in_specs=in_specs,
        out_specs=pl.BlockSpec((tm, tn), lambda j, i: (i, j)),
        out_shape=jax.ShapeDtypeStruct((T, n_cols), BF16),
        scratch_shapes=[pltpu.VMEM((D, tn), BF16)],
        compiler_params=_params("parallel", "arbitrary"),
        name="qk_proj" if cos is not None else "v_proj",
    )(*args)


def _attn_kernel(scal_ref, lam_ref, subln_ref, q_ref, k_ref, v_ref, o_ref,
                 vt_ref, qt_ref, sa_ref, sb_ref, xa_ref, xb_ref,
                 m1_ref, l1_ref, a1_ref, m2_ref, l2_ref, a2_ref,
                 *, blk, hd):
    nq = vt_ref.shape[0]
    unroll = 4 if nq % 4 == 0 else nq
    c = hd ** -0.5 * math.log2(math.e)

    for n in range(nq):
        rows = slice(n * blk, (n + 1) * blk)
        vt_ref[n] = v_ref[rows, :].T
        qt_ref[n, 0] = q_ref[rows, :hd].T
        qt_ref[n, 1] = q_ref[rows, hd:].T

    lp = lam_ref[...]
    lam_init = scal_ref[0]
    lam_full = (jnp.exp(jnp.sum(lp[0:1] * lp[1:2], keepdims=True))
                - jnp.exp(jnp.sum(lp[2:3] * lp[3:4], keepdims=True)) + lam_init)

    def update(s, smax, vt, m_ref, l_ref, a_ref):
        m_old = m_ref[...]
        m_new = jnp.maximum(m_old, smax)
        alpha = jnp.exp2((m_old - m_new) * c)
        p = jnp.exp2((s - m_new) * c)
        l_ref[...] = alpha * l_ref[...] + jnp.sum(p, axis=0, keepdims=True)
        a_ref[...] = alpha * a_ref[...] + _mm(vt, p.astype(vt.dtype))
        m_ref[...] = m_new

    def init_stats():
        for m_ref, l_ref, a_ref in ((m1_ref, l1_ref, a1_ref), (m2_ref, l2_ref, a2_ref)):
            m_ref[...] = jnp.full(m_ref.shape, -jnp.inf, F32)
            l_ref[...] = jnp.zeros(l_ref.shape, F32)
            a_ref[...] = jnp.zeros(a_ref.shape, F32)

    def scores(qi, j, slot):
        buf, mx = slot
        k = k_ref[pl.ds(pl.multiple_of(j * blk, blk), blk), :]
        for i in range(2):
            s = _mm(k[:, i * hd:(i + 1) * hd], qt_ref[qi, i])
            buf[i] = s
            mx[i] = jnp.max(s, axis=0, keepdims=True)

    def process(j, slot, masked):
        buf, mx = slot
        vt = vt_ref[j]
        for i, (m_ref, l_ref, a_ref) in enumerate(((m1_ref, l1_ref, a1_ref),
                                                   (m2_ref, l2_ref, a2_ref))):
            s = buf[i]
            if masked:
                r = lax.broadcasted_iota(jnp.int32, s.shape, 0)
                q = lax.broadcasted_iota(jnp.int32, s.shape, 1)
                s = jnp.where(r <= q, s, -jnp.inf)
                smax = jnp.max(s, axis=0, keepdims=True)
            else:
                smax = mx[i]
            update(s, smax, vt, m_ref, l_ref, a_ref)

    def finish(qi):
        o = a1_ref[...] / l1_ref[...] - lam_full * (a2_ref[...] / l2_ref[...])
        init_stats()
        y = o * lax.rsqrt(jnp.mean(o * o, axis=0, keepdims=True) + EPS)
        y = (y * subln_ref[...]) * (1.0 - lam_init)
        o_ref[pl.ds(pl.multiple_of(qi * blk, blk), blk), :] = y.T.astype(o_ref.dtype)

    def q_block(qi, odd, cur, oth):
        def body(jj, carry):
            j = 2 * jj
            scores(qi, j + 1, oth)
            process(j, cur, masked=False)
            scores(qi, j + 2, cur)
            process(j + 1, oth, masked=False)
            return carry

        lax.fori_loop(0, qi // 2, body, 0)
        nxt = jnp.minimum(qi + 1, nq - 1)
        if odd:
            scores(qi, qi, oth)
            process(qi - 1, cur, masked=False)
            scores(nxt, 0, cur)
            process(qi, oth, masked=True)
            first = cur
        else:
            scores(nxt, 0, oth)
            process(qi, cur, masked=True)
            first = oth
        finish(qi)
        return first

    slot_a, slot_b = (sa_ref, xa_ref), (sb_ref, xb_ref)

    def q_group(g, carry):
        cur = slot_a
        for r in range(unroll):
            oth = slot_b if cur is slot_a else slot_a
            cur = q_block(g * unroll + r, r % 2 == 1, cur, oth)
        assert cur is slot_a or nq == unroll
        return carry

    init_stats()
    scores(0, 0, slot_a)
    if nq == unroll:
        q_group(0, 0)
    else:
        lax.fori_loop(0, nq // unroll, q_group, 0)


def _diff_attention(qk, v, lam_l, subln_l, lam_init, B, S, H, hd, blk_pref=512):
    vhd = 2 * hd
    blk = _blk(S, blk_pref)
    qk3 = qk.reshape(B, S, 2 * H * vhd)
    v3 = v.reshape(B, S, H * vhd)
    scal = jnp.full((1,), lam_init, F32)
    stat = pltpu.VMEM((1, blk), F32)
    acc = pltpu.VMEM((vhd, blk), F32)
    scores = pltpu.VMEM((2, blk, blk), F32)
    nq = S // blk
    return pl.pallas_call(
        functools.partial(_attn_kernel, blk=blk, hd=hd),
        grid=(B, H),
        in_specs=[pl.BlockSpec(memory_space=pltpu.SMEM),
                  pl.BlockSpec((4, hd), lambda b, h: (0, 0)),
                  pl.BlockSpec((vhd, 1), lambda b, h: (0, 0)),
                  pl.BlockSpec((None, S, vhd), lambda b, h: (b, 0, h)),
                  pl.BlockSpec((None, S, vhd), lambda b, h: (b, 0, H + h)),
                  pl.BlockSpec((None, S, vhd), lambda b, h: (b, 0, h))],
        out_specs=pl.BlockSpec((None, S, vhd), lambda b, h: (b, 0, h)),
        out_shape=jax.ShapeDtypeStruct((B, S, H * vhd), BF16),
        scratch_shapes=[pltpu.VMEM((nq, vhd, blk), BF16), pltpu.VMEM((nq, 2, hd, blk), BF16),
                        scores, scores, pltpu.VMEM((2, 1, blk), F32), pltpu.VMEM((2, 1, blk), F32),
                        stat, stat, acc, stat, stat, acc],
        compiler_params=_params("parallel", "parallel"),
        name="diff_attention",
    )(scal, lam_l, subln_l.reshape(vhd, 1), qk3, qk3, v3).reshape(B * S, H * vhd)


def _cast_rows(w_ref, wb_ref, n_cast):
    i = pl.program_id(0)

    @pl.when(i < n_cast)
    def _():
        rc = w_ref.shape[0]
        wb_ref[pl.ds(pl.multiple_of(i * rc, rc), rc), :] = w_ref[...].astype(wb_ref.dtype)


def _sgu_kernel(h_ref, wu_ref, wv_ref, lng_ref, lnb_ref, ws_ref, bs_ref, o_ref,
                wub_ref, wvb_ref, u_ref, *, chunk, groups, n_cast):
    _cast_rows(wu_ref, wub_ref, n_cast)
    _cast_rows(wv_ref, wvb_ref, n_cast)

    @pl.when(pl.program_id(0) >= n_cast)
    def _():
        W = o_ref.shape[1]
        gd = W // groups
        cw = _blk(W, 512)
        h = h_ref[...]

        def gelu_cols(wb_ref, c0):
            z = _mm(h, wb_ref[:, c0:c0 + cw])
            return 0.5 * z * (1.0 + lax.erf(z * (2.0 ** -0.5)))

        vg = jnp.concatenate([gelu_cols(wvb_ref, c0) for c0 in range(0, W, cw)], axis=1)
        mu = jnp.mean(vg, axis=-1, keepdims=True)
        d = vg - mu
        var = jnp.mean(d * d, axis=-1, keepdims=True)
        vn = ((d * lax.rsqrt(var + EPS)) * lng_ref[...] + lnb_ref[...]).astype(BF16)
        for c0 in range(0, W, cw):
            u_ref[:, c0:c0 + cw] = gelu_cols(wub_ref, c0)
        r = lax.broadcasted_iota(jnp.int32, (chunk, chunk), 0)
        c = lax.broadcasted_iota(jnp.int32, (chunk, chunk), 1)
        causal = r >= c
        for g in range(groups):
            w = jnp.where(causal, ws_ref[g], 0.0).astype(BF16)
            bias = bs_ref[:, g:g + 1]
            cols = slice(g * gd, (g + 1) * gd)
            for n in range(o_ref.shape[0] // chunk):
                rows = slice(n * chunk, (n + 1) * chunk)
                mixed = _mm(w, vn[rows, cols]) + bias
                o_ref[rows, cols] = (u_ref[rows, cols] * mixed).astype(o_ref.dtype)


def _sgu(h, w_in, layer, col0, ln_g, ln_b, ws, bs_t, chunk, groups):
    T, D = h.shape
    W = ln_g.shape[0]
    assert col0 % W == 0
    tm = 2 * chunk if T % (2 * chunk) == 0 else chunk
    rc = _blk(D, 256)
    n_cast = D // rc
    row = lambda i: (jnp.maximum(i - n_cast, 0), 0)
    wrow = lambda half: (lambda i: (layer, jnp.minimum(i, n_cast - 1), col0 // W + half))
    return pl.pallas_call(
        functools.partial(_sgu_kernel, chunk=chunk, groups=groups, n_cast=n_cast),
        grid=(n_cast + T // tm,),
        in_specs=[pl.BlockSpec((tm, D), row),
                  pl.BlockSpec((None, rc, W), wrow(0)),
                  pl.BlockSpec((None, rc, W), wrow(1)),
                  pl.BlockSpec((1, W), lambda i: (0, 0)),
                  pl.BlockSpec((1, W), lambda i: (0, 0)),
                  pl.BlockSpec((groups, chunk, chunk), lambda i: (0, 0, 0)),
                  pl.BlockSpec((chunk, groups), lambda i: (0, 0))],
        out_specs=pl.BlockSpec((tm, W), row),
        out_shape=jax.ShapeDtypeStruct((T, W), BF16),
        scratch_shapes=[pltpu.VMEM((D, W), BF16), pltpu.VMEM((D, W), BF16),
                        pltpu.VMEM((tm, W), F32)],
        compiler_params=_params("arbitrary"),
        name="sgu",
    )(h, w_in, w_in, ln_g.reshape(1, W), ln_b.reshape(1, W), ws, bs_t)


def _merge_kernel(h_ref, o_ref, sg_ref, wga_ref, wgs_ref, wa_ref, ws_ref, ba_ref, bs_ref, out_ref,
                  wgab_ref, wgsb_ref, wab_ref, wsb_ref):
    for w_ref, wb_ref in ((wga_ref, wgab_ref), (wgs_ref, wgsb_ref), (wa_ref, wab_ref),
                          (ws_ref, wsb_ref)):
        _cache_bf16(w_ref, wb_ref)
    h = h_ref[...]
    g_attn = jax.nn.sigmoid(_mm(h, wgab_ref[...]) + ba_ref[...])
    g_sgu = jax.nn.sigmoid(_mm(h, wgsb_ref[...]) + bs_ref[...])
    y = g_attn * _mm(o_ref[...], wab_ref[...]) + g_sgu * _mm(sg_ref[...], wsb_ref[...])
    out_ref[...] = y.astype(out_ref.dtype)


def _merge(h, o, sg, w_in, layer, gate_col0, gate_b, w_branch):
    T, D = h.shape
    tm, tn = _blk(T, 1024), _blk(D, 256)
    nb = D // tn
    assert gate_col0 % tn == 0
    g0 = gate_col0 // tn
    row = lambda j, i: (i, 0)
    return pl.pallas_call(
        _merge_kernel,
        grid=(nb, T // tm),
        in_specs=[pl.BlockSpec((tm, D), row),
                  pl.BlockSpec((tm, o.shape[1]), row),
                  pl.BlockSpec((tm, sg.shape[1]), row),
                  pl.BlockSpec((None, D, tn), lambda j, i: (layer, 0, g0 + j)),
                  pl.BlockSpec((None, D, tn), lambda j, i: (layer, 0, g0 + nb + j)),
                  pl.BlockSpec((None, None, o.shape[1], tn), lambda j, i: (layer, 0, 0, j)),
                  pl.BlockSpec((None, None, sg.shape[1], tn), lambda j, i: (layer, 1, 0, j)),
                  pl.BlockSpec((1, tn), lambda j, i: (0, j)),
                  pl.BlockSpec((1, tn), lambda j, i: (0, nb + j))],
        out_specs=pl.BlockSpec((tm, tn), lambda j, i: (i, j)),
        out_shape=jax.ShapeDtypeStruct((T, D), BF16),
        scratch_shapes=[pltpu.VMEM((D, tn), BF16), pltpu.VMEM((D, tn), BF16),
                        pltpu.VMEM((o.shape[1], tn), BF16), pltpu.VMEM((sg.shape[1], tn), BF16)],
        compiler_params=_params("parallel", "arbitrary"),
        name="merge",
    )(h, o, sg, w_in, w_in, w_branch, w_branch, gate_b.reshape(1, -1), gate_b.reshape(1, -1))


def _proj_res_norm_kernel(a_ref, w_ref, x_ref, g_ref, *refs, n_cast):
    *out_refs, wb_ref = refs
    _cast_rows(w_ref, wb_ref, n_cast)

    @pl.when(pl.program_id(0) >= n_cast)
    def _():
        rc = _blk(x_ref.shape[0], ROW_CHUNK)
        for r0 in range(0, x_ref.shape[0], rc):
            rows = slice(r0, r0 + rc)
            y = x_ref[rows, :] + _mm(a_ref[rows, :], wb_ref[...])
            if len(out_refs) == 2:
                out_refs[0][rows, :] = y
            n = y * lax.rsqrt(jnp.mean(y * y, axis=-1, keepdims=True) + EPS)
            out_refs[-1][rows, :] = (n * g_ref[...]).astype(out_refs[-1].dtype)


def _proj_res_norm(a, w_all, layer, x, g, norm_dtype, emit_x=True):
    T, K = a.shape
    D = w_all.shape[2]
    tm = _blk(T, 512 if K <= D else 256)
    rc = _blk(K, 512)
    n_cast = K // rc
    row = lambda i: (jnp.maximum(i - n_cast, 0), 0)
    out_specs = [pl.BlockSpec((tm, D), row)] * (2 if emit_x else 1)
    out_shape = ([jax.ShapeDtypeStruct((T, D), F32)] if emit_x else []) + [
        jax.ShapeDtypeStruct((T, D), norm_dtype)]
    outs = pl.pallas_call(
        functools.partial(_proj_res_norm_kernel, n_cast=n_cast),
        grid=(n_cast + T // tm,),
        in_specs=[pl.BlockSpec((tm, K), row),
                  pl.BlockSpec((None, rc, D), lambda i: (layer, jnp.minimum(i, n_cast - 1), 0)),
                  pl.BlockSpec((tm, D), row),
                  pl.BlockSpec((1, D), lambda i: (0, 0))],
        out_specs=out_specs,
        out_shape=out_shape,
        scratch_shapes=[pltpu.VMEM((K, D), BF16)],
        compiler_params=_params("arbitrary"),
        name="proj_res_norm",
    )(a, w_all, x, g.reshape(1, D))
    return outs if emit_x else outs[0]


def _ffn_in_kernel(h_ref, wa_ref, wb_ref, o_ref, wab_ref, wbb_ref):
    _cache_bf16(wa_ref, wab_ref)
    _cache_bf16(wb_ref, wbb_ref)
    h = h_ref[...]
    a = _mm(h, wab_ref[...])
    b = _mm(h, wbb_ref[...])
    o_ref[...] = (a * jax.nn.sigmoid(a) * b).astype(o_ref.dtype)


def _ffn_in(h, w_ffn_in, layer):
    T, D = h.shape
    F = w_ffn_in.shape[2] // 2
    tm, tn = _blk(T, 1024), _blk(F, 512)
    nb = F // tn
    return pl.pallas_call(
        _ffn_in_kernel,
        grid=(nb, T // tm),
        in_specs=[pl.BlockSpec((tm, D), lambda j, i: (i, 0)),
                  pl.BlockSpec((None, D, tn), lambda j, i: (layer, 0, j)),
                  pl.BlockSpec((None, D, tn), lambda j, i: (layer, 0, nb + j))],
        out_specs=pl.BlockSpec((tm, tn), lambda j, i: (i, j)),
        out_shape=jax.ShapeDtypeStruct((T, F), BF16),
        scratch_shapes=[pltpu.VMEM((D, tn), BF16)] * 2,
        compiler_params=_params("parallel", "arbitrary"),
        name="ffn_in",
    )(h, w_ffn_in, w_ffn_in)


def kernel(x, positions, norm_attn, w_in, gate_b, lam, subln, sgu_ln_g, sgu_ln_b, w_spatial,
           b_spatial, w_branch, w_out, norm_ffn, w_ffn_in, w_ffn_out, norm_final):
    B, S, D = x.shape
    T = B * S
    depth = w_in.shape[0]
    hd = lam.shape[-1]
    H = D // subln.shape[-1]
    groups, chunk = w_spatial.shape[1], w_spatial.shape[2]
    W = sgu_ln_g.shape[-1]
    assert S % chunk == 0 and w_in.shape[2] == 3 * D + 2 * W + 2 * D

    inv_freq = ROPE_THETA ** (-jnp.arange(0, hd, 2, dtype=F32) / hd)
    invf = jnp.concatenate([inv_freq, inv_freq]).reshape(1, hd)
    cos, sin = _rope_tables(positions.reshape(T, 1), invf)

    xf = x.reshape(T, D)
    h = _rmsnorm(xf, norm_attn[0], BF16)
    for l in range(depth):
        lam_init = 0.8 - 0.6 * math.exp(-0.3 * l)
        qk = _in_proj(h, w_in, l, 0, 2 * D, cos, sin)
        v = _in_proj(h, w_in, l, 2 * D, D)
        o = _diff_attention(qk, v, lam[l], subln[l], lam_init, B, S, H, hd)
        sg = _sgu(h, w_in, l, 3 * D, sgu_ln_g[l], sgu_ln_b[l], w_spatial[l], b_spatial[l].T,
                  chunk, groups)
        merged = _merge(h, o, sg, w_in, l, 3 * D + 2 * W, gate_b[l], w_branch)
        xf, h2 = _proj_res_norm(merged, w_out, l, xf, norm_ffn[l], BF16)

        act = _ffn_in(h2, w_ffn_in, l)
        if l + 1 < depth:
            xf, h = _proj_res_norm(act, w_ffn_out, l, xf, norm_attn[l + 1], BF16)
        else:
            out = _proj_res_norm(act, w_ffn_out, l, xf, norm_final, x.dtype, emit_x=False)
    return out.reshape(B, S, D)
```

```python
import functools
import math

import jax
import jax.numpy as jnp
from jax import lax
from jax.experimental import pallas as pl
from jax.experimental.pallas import tpu as pltpu

ROPE_THETA = 10000.0
EPS = 1e-6
LANES = 128
MXU_COLS = 256
ROW_CHUNK = 256
DOT_ROWS = 1024
VMEM_LIMIT_BYTES = 56 * 1024 * 1024

F32 = jnp.float32
BF16 = jnp.bfloat16


def _blk(dim, pref):
    if dim <= pref:
        return dim
    b = pref - pref % LANES
    while b > LANES and dim % b:
        b -= LANES
    assert dim % b == 0, (dim, pref)
    return b


def _params(*sem):
    return pltpu.CompilerParams(dimension_semantics=sem, vmem_limit_bytes=VMEM_LIMIT_BYTES)


def _mm(a, b):
    return jnp.dot(a, b, preferred_element_type=F32)


def _rope_kernel(pos_ref, invf_ref, cos_ref, sin_ref):
    ang = pos_ref[...].astype(F32) * invf_ref[...]
    lane = lax.broadcasted_iota(jnp.int32, ang.shape, 1)
    sin = jnp.sin(ang)
    cos_ref[...] = jnp.cos(ang)
    sin_ref[...] = jnp.where(lane < ang.shape[1] // 2, -sin, sin)


def _rope_tables(pos, invf):
    T = pos.shape[0]
    hd = invf.shape[1]
    tm = _blk(T, 2048)
    return pl.pallas_call(
        _rope_kernel,
        grid=(T // tm,),
        in_specs=[pl.BlockSpec((tm, 1), lambda i: (i, 0)),
                  pl.BlockSpec((1, hd), lambda i: (0, 0))],
        out_specs=[pl.BlockSpec((tm, hd), lambda i: (i, 0)),
                   pl.BlockSpec((tm, hd), lambda i: (i, 0))],
        out_shape=[jax.ShapeDtypeStruct((T, hd), F32)] * 2,
        compiler_params=_params("parallel"),
        name="rope_tables",
    )(pos, invf)


def _rmsnorm_kernel(x_ref, g_ref, o_ref):
    x = x_ref[...]
    y = x * lax.rsqrt(jnp.mean(x * x, axis=-1, keepdims=True) + EPS)
    o_ref[...] = (y * g_ref[...]).astype(o_ref.dtype)


def _rmsnorm(x, g, out_dtype):
    T, D = x.shape
    tm = _blk(T, 512)
    return pl.pallas_call(
        _rmsnorm_kernel,
        grid=(T // tm,),
        in_specs=[pl.BlockSpec((tm, D), lambda i: (i, 0)),
                  pl.BlockSpec((1, D), lambda i: (0, 0))],
        out_specs=pl.BlockSpec((tm, D), lambda i: (i, 0)),
        out_shape=jax.ShapeDtypeStruct((T, D), out_dtype),
        compiler_params=_params("parallel"),
        name="rmsnorm",
    )(x, g.reshape(1, D))


def _cache_bf16(w_ref, wb_ref):
    @pl.when(pl.program_id(1) == 0)
    def _():
        wb_ref[...] = w_ref[...].astype(wb_ref.dtype)


def _qk_kernel(h_ref, w_ref, cos_ref, sin_ref, o_ref, wb_ref, *, hd):
    _cache_bf16(w_ref, wb_ref)
    rc = _blk(o_ref.shape[0], ROW_CHUNK)
    for r0 in range(0, o_ref.shape[0], rc):
        rows = slice(r0, r0 + rc)
        acc = _mm(h_ref[rows, :], wb_ref[...])
        cos = cos_ref[rows, :]
        sin = sin_ref[rows, :]
        for s0 in range(0, o_ref.shape[1], hd):
            t = acc[:, s0:s0 + hd]
            r = t * cos + pltpu.roll(t, hd // 2, 1) * sin
            o_ref[rows, s0:s0 + hd] = r.astype(o_ref.dtype)


def _v_kernel(h_ref, w_ref, o_ref, wb_ref):
    _cache_bf16(w_ref, wb_ref)
    rc = _blk(o_ref.shape[0], DOT_ROWS)
    for r0 in range(0, o_ref.shape[0], rc):
        rows = slice(r0, r0 + rc)
        o_ref[rows, :] = _mm(h_ref[rows, :], wb_ref[...]).astype(o_ref.dtype)


def _in_proj(h, w_in, layer, col0, n_cols, cos=None, sin=None):
    T, D = h.shape
    tm, tn = _blk(T, 2 * DOT_ROWS), _blk(n_cols, 1024)
    assert col0 % tn == 0
    c0 = col0 // tn
    in_specs = [pl.BlockSpec((tm, D), lambda j, i: (i, 0)),
                pl.BlockSpec((None, D, tn), lambda j, i: (layer, 0, c0 + j))]
    args = [h, w_in]
    if cos is None:
        body = _v_kernel
    else:
        hd = cos.shape[1]
        body = functools.partial(_qk_kernel, hd=hd)
        in_specs += [pl.BlockSpec((tm, hd), lambda j, i: (i, 0))] * 2
        args += [cos, sin]
    return pl.pallas_call(
        body,
        grid=(n_cols // tn, T // tm),
        in_specs=in_specs,
        out_specs=pl.BlockSpec((tm, tn), lambda j, i: (i, j)),
        out_shape=jax.ShapeDtypeStruct((T, n_cols), BF16),
        scratch_shapes=[pltpu.VMEM((D, tn), BF16)],
        compiler_params=_params("parallel", "arbitrary"),
        name="qk_proj" if cos is not None else "v_proj",
    )(*args)


def _attn_kernel(scal_ref, lam_ref, subln_ref, q_ref, k_ref, v_ref, o_ref,
                 vt_ref, qt_ref, sa_ref, sb_ref, xa_ref, xb_ref,
                 m1_ref, l1_ref, a1_ref, m2_ref, l2_ref, a2_ref,
                 *, blk, hd):
    nq = vt_ref.shape[0]
    unroll = 4 if nq % 4 == 0 else nq
    c = hd ** -0.5 * math.log2(math.e)

    for n in range(nq):
        rows = slice(n * blk, (n + 1) * blk)
        vt_ref[n] = v_ref[rows, :].T
        qt_ref[n, 0] = q_ref[rows, :hd].T
        qt_ref[n, 1] = q_ref[rows, hd:].T

    lp = lam_ref[...]
    lam_init = scal_ref[0]
    lam_full = (jnp.exp(jnp.sum(lp[0:1] * lp[1:2], keepdims=True))
                - jnp.exp(jnp.sum(lp[2:3] * lp[3:4], keepdims=True)) + lam_init)

    def update(s, smax, vt, m_ref, l_ref, a_ref):
        m_old = m_ref[...]
        m_new = jnp.maximum(m_old, smax)
        alpha = jnp.exp2((m_old - m_new) * c)
        p = jnp.exp2((s - m_new) * c)
        l_ref[...] = alpha * l_ref[...] + jnp.sum(p, axis=0, keepdims=True)
        a_ref[...] = alpha * a_ref[...] + _mm(vt, p.astype(vt.dtype))
        m_ref[...] = m_new

    def init_stats():
        for m_ref, l_ref, a_ref in ((m1_ref, l1_ref, a1_ref), (m2_ref, l2_ref, a2_ref)):
            m_ref[...] = jnp.full(m_ref.shape, -jnp.inf, F32)
            l_ref[...] = jnp.zeros(l_ref.shape, F32)
            a_ref[...] = jnp.zeros(a_ref.shape, F32)

    def scores(qi, j, slot):
        buf, mx = slot
        k = k_ref[pl.ds(pl.multiple_of(j * blk, blk), blk), :]
        for i in range(2):
            s = _mm(k[:, i * hd:(i + 1) * hd], qt_ref[qi, i])
            buf[i] = s
            mx[i] = jnp.max(s, axis=0, keepdims=True)

    def process(j, slot, masked):
        buf, mx = slot
        vt = vt_ref[j]
        for i, (m_ref, l_ref, a_ref) in enumerate(((m1_ref, l1_ref, a1_ref),
                                                   (m2_ref, l2_ref, a2_ref))):
            s = buf[i]
            if masked:
                r = lax.broadcasted_iota(jnp.int32, s.shape, 0)
                q = lax.broadcasted_iota(jnp.int32, s.shape, 1)
                s = jnp.where(r <= q, s, -jnp.inf)
                smax = jnp.max(s, axis=0, keepdims=True)
            else:
                smax = mx[i]
            update(s, smax, vt, m_ref, l_ref, a_ref)

    def finish(qi):
        o = a1_ref[...] / l1_ref[...] - lam_full * (a2_ref[...] / l2_ref[...])
        init_stats()
        y = o * lax.rsqrt(jnp.mean(o * o, axis=0, keepdims=True) + EPS)
        y = (y * subln_ref[...]) * (1.0 - lam_init)
        o_ref[pl.ds(pl.multiple_of(qi * blk, blk), blk), :] = y.T.astype(o_ref.dtype)

    def q_block(qi, odd, cur, oth):
        def body(jj, carry):
            j = 2 * jj
            scores(qi, j + 1, oth)
            process(j, cur, masked=False)
            scores(qi, j + 2, cur)
            process(j + 1, oth, masked=False)
            return carry

        lax.fori_loop(0, qi // 2, body, 0)
        nxt = jnp.minimum(qi + 1, nq - 1)
        if odd:
            scores(qi, qi, oth)
            process(qi - 1, cur, masked=False)
            scores(nxt, 0, cur)
            process(qi, oth, masked=True)
            first = cur
        else:
            scores(nxt, 0, oth)
            process(qi, cur, masked=True)
            first = oth
        finish(qi)
        return first

    slot_a, slot_b = (sa_ref, xa_ref), (sb_ref, xb_ref)

    def q_group(g, carry):
        cur = slot_a
        for r in range(unroll):
            oth = slot_b if cur is slot_a else slot_a
            cur = q_block(g * unroll + r, r % 2 == 1, cur, oth)
        assert cur is slot_a or nq == unroll
        return carry

    init_stats()
    scores(0, 0, slot_a)
    if nq == unroll:
        q_group(0, 0)
    else:
        lax.fori_loop(0, nq // unroll, q_group, 0)


def _diff_attention(qk, v, lam_l, subln_l, lam_init, B, S, H, hd, blk_pref=512):
    vhd = 2 * hd
    blk = _blk(S, blk_pref)
    qk3 = qk.reshape(B, S, 2 * H * vhd)
    v3 = v.reshape(B, S, H * vhd)
    scal = jnp.full((1,), lam_init, F32)
    stat = pltpu.VMEM((1, blk), F32)
    acc = pltpu.VMEM((vhd, blk), F32)
    scores = pltpu.VMEM((2, blk, blk), F32)
    nq = S // blk
    return pl.pallas_call(
        functools.partial(_attn_kernel, blk=blk, hd=hd),
        grid=(B, H),
        in_specs=[pl.BlockSpec(memory_space=pltpu.SMEM),
                  pl.BlockSpec((4, hd), lambda b, h: (0, 0)),
                  pl.BlockSpec((vhd, 1), lambda b, h: (0, 0)),
                  pl.BlockSpec((None, S, vhd), lambda b, h: (b, 0, h)),
                  pl.BlockSpec((None, S, vhd), lambda b, h: (b, 0, H + h)),
                  pl.BlockSpec((None, S, vhd), lambda b, h: (b, 0, h))],
        out_specs=pl.BlockSpec((None, S, vhd), lambda b, h: (b, 0, h)),
        out_shape=jax.ShapeDtypeStruct((B, S, H * vhd), BF16),
        scratch_shapes=[pltpu.VMEM((nq, vhd, blk), BF16), pltpu.VMEM((nq, 2, hd, blk), BF16),
                        scores, scores, pltpu.VMEM((2, 1, blk), F32), pltpu.VMEM((2, 1, blk), F32),
                        stat, stat, acc, stat, stat, acc],
        compiler_params=_params("parallel", "parallel"),
        name="diff_attention",
    )(scal, lam_l, subln_l.reshape(vhd, 1), qk3, qk3, v3).reshape(B * S, H * vhd)


def _cast_rows(w_ref, wb_ref, n_cast):
    i = pl.program_id(0)

    @pl.when(i < n_cast)
    def _():
        rc = w_ref.shape[0]
        wb_ref[pl.ds(pl.multiple_of(i * rc, rc), rc), :] = w_ref[...].astype(wb_ref.dtype)


def _sgu_kernel(h_ref, wu_ref, wv_ref, lng_ref, lnb_ref, ws_ref, bs_ref, o_ref,
                wub_ref, wvb_ref, u_ref, *, chunk, groups, n_cast):
    _cast_rows(wu_ref, wub_ref, n_cast)
    _cast_rows(wv_ref, wvb_ref, n_cast)

    @pl.when(pl.program_id(0) >= n_cast)
    def _():
        W = o_ref.shape[1]
        gd = W // groups
        cw = _blk(W, 512)
        r = lax.broadcasted_iota(jnp.int32, (chunk, chunk), 0)
        c = lax.broadcasted_iota(jnp.int32, (chunk, chunk), 1)
        causal = r >= c
        sub = u_ref.shape[1]

        for sb in range(u_ref.shape[0]):
            b0 = sb * sub
            u_sb = u_ref.at[sb]
            h = h_ref[b0:b0 + sub, :]

            def gelu_cols(wb_ref, c0):
                z = _mm(h, wb_ref[:, c0:c0 + cw])
                return 0.5 * z * (1.0 + lax.erf(z * (2.0 ** -0.5)))

            vg = jnp.concatenate([gelu_cols(wvb_ref, c0) for c0 in range(0, W, cw)], axis=1)
            mu = jnp.mean(vg, axis=-1, keepdims=True)
            d = vg - mu
            var = jnp.mean(d * d, axis=-1, keepdims=True)
            vn = ((d * lax.rsqrt(var + EPS)) * lng_ref[...] + lnb_ref[...]).astype(BF16)
            for c0 in range(0, W, cw):
                u_sb[:, c0:c0 + cw] = gelu_cols(wub_ref, c0)
            for g in range(groups):
                w = jnp.where(causal, ws_ref[g], 0.0).astype(BF16)
                bias = bs_ref[:, g:g + 1]
                cols = slice(g * gd, (g + 1) * gd)
                for n in range(sub // chunk):
                    rows = slice(n * chunk, (n + 1) * chunk)
                    mixed = _mm(w, vn[rows, cols]) + bias
                    o_ref[b0 + n * chunk:b0 + (n + 1) * chunk, cols] = (
                        u_sb[rows, cols] * mixed).astype(o_ref.dtype)


def _sgu(h, w_in, layer, col0, ln_g, ln_b, ws, bs_t, chunk, groups):
    T, D = h.shape
    W = ln_g.shape[0]
    assert col0 % W == 0
    sub = 2 * chunk if T % (2 * chunk) == 0 else chunk
    tm = 2 * sub if T % (2 * sub) == 0 else sub
    rc = _blk(D, 256)
    n_cast = D // rc
    row = lambda i: (jnp.maximum(i - n_cast, 0), 0)
    wrow = lambda half: (lambda i: (layer, jnp.minimum(i, n_cast - 1), col0 // W + half))
    return pl.pallas_call(
        functools.partial(_sgu_kernel, chunk=chunk, groups=groups, n_cast=n_cast),
        grid=(n_cast + T // tm,),
        in_specs=[pl.BlockSpec((tm, D), row),
                  pl.BlockSpec((None, rc, W), wrow(0)),
                  pl.BlockSpec((None, rc, W), wrow(1)),
                  pl.BlockSpec((1, W), lambda i: (0, 0)),
                  pl.BlockSpec((1, W), lambda i: (0, 0)),
                  pl.BlockSpec((groups, chunk, chunk), lambda i: (0, 0, 0)),
                  pl.BlockSpec((chunk, groups), lambda i: (0, 0))],
        out_specs=pl.BlockSpec((tm, W), row),
        out_shape=jax.ShapeDtypeStruct((T, W), BF16),
        scratch_shapes=[pltpu.VMEM((D, W), BF16), pltpu.VMEM((D, W), BF16),
                        pltpu.VMEM((tm // sub, sub, W), F32)],
        compiler_params=_params("arbitrary"),
        name="sgu",
    )(h, w_in, w_in, ln_g.reshape(1, W), ln_b.reshape(1, W), ws, bs_t)


def _merge_kernel(h_ref, o_ref, sg_ref, wga_ref, wgs_ref, wa_ref, ws_ref, ba_ref, bs_ref, out_ref,
                  wgab_ref, wgsb_ref, wab_ref, wsb_ref):
    for w_ref, wb_ref in ((wga_ref, wgab_ref), (wgs_ref, wgsb_ref), (wa_ref, wab_ref),
                          (ws_ref, wsb_ref)):
        _cache_bf16(w_ref, wb_ref)
    h = h_ref[...]
    g_attn = jax.nn.sigmoid(_mm(h, wgab_ref[...]) + ba_ref[...])
    g_sgu = jax.nn.sigmoid(_mm(h, wgsb_ref[...]) + bs_ref[...])
    y = g_attn * _mm(o_ref[...], wab_ref[...]) + g_sgu * _mm(sg_ref[...], wsb_ref[...])
    out_ref[...] = y.astype(out_ref.dtype)


def _merge(h, o, sg, w_in, layer, gate_col0, gate_b, w_branch):
    T, D = h.shape
    tm, tn = _blk(T, 1024), _blk(D, 256)
    nb = D // tn
    assert gate_col0 % tn == 0
    g0 = gate_col0 // tn
    row = lambda j, i: (i, 0)
    return pl.pallas_call(
        _merge_kernel,
        grid=(nb, T // tm),
        in_specs=[pl.BlockSpec((tm, D), row),
                  pl.BlockSpec((tm, o.shape[1]), row),
                  pl.BlockSpec((tm, sg.shape[1]), row),
                  pl.BlockSpec((None, D, tn), lambda j, i: (layer, 0, g0 + j)),
                  pl.BlockSpec((None, D, tn), lambda j, i: (layer, 0, g0 + nb + j)),
                  pl.BlockSpec((None, None, o.shape[1], tn), lambda j, i: (layer, 0, 0, j)),
                  pl.BlockSpec((None, None, sg.shape[1], tn), lambda j, i: (layer, 1, 0, j)),
                  pl.BlockSpec((1, tn), lambda j, i: (0, j)),
                  pl.BlockSpec((1, tn), lambda j, i: (0, nb + j))],
        out_specs=pl.BlockSpec((tm, tn), lambda j, i: (i, j)),
        out_shape=jax.ShapeDtypeStruct((T, D), BF16),
        scratch_shapes=[pltpu.VMEM((D, tn), BF16), pltpu.VMEM((D, tn), BF16),
                        pltpu.VMEM((o.shape[1], tn), BF16), pltpu.VMEM((sg.shape[1], tn), BF16)],
        compiler_params=_params("parallel", "arbitrary"),
        name="merge",
    )(h, o, sg, w_in, w_in, w_branch, w_branch, gate_b.reshape(1, -1), gate_b.reshape(1, -1))


def _proj_res_norm_kernel(a_ref, w_ref, x_ref, g_ref, *refs, n_cast):
    *out_refs, wb_ref = refs
    _cast_rows(w_ref, wb_ref, n_cast)

    @pl.when(pl.program_id(0) >= n_cast)
    def _():
        rc = _blk(x_ref.shape[0], ROW_CHUNK)
        for r0 in range(0, x_ref.shape[0], rc):
            rows = slice(r0, r0 + rc)
            y = x_ref[rows, :] + _mm(a_ref[rows, :], wb_ref[...])
            if len(out_refs) == 2:
                out_refs[0][rows, :] = y
            n = y * lax.rsqrt(jnp.mean(y * y, axis=-1, keepdims=True) + EPS)
            out_refs[-1][rows, :] = (n * g_ref[...]).astype(out_refs[-1].dtype)


def _proj_res_norm(a, w_all, layer, x, g, norm_dtype, emit_x=True):
    T, K = a.shape
    D = w_all.shape[2]
    tm = _blk(T, 512 if K <= D else 256)
    rc = _blk(K, 512)
    n_cast = K // rc
    row = lambda i: (jnp.maximum(i - n_cast, 0), 0)
    out_specs = [pl.BlockSpec((tm, D), row)] * (2 if emit_x else 1)
    out_shape = ([jax.ShapeDtypeStruct((T, D), F32)] if emit_x else []) + [
        jax.ShapeDtypeStruct((T, D), norm_dtype)]
    outs = pl.pallas_call(
        functools.partial(_proj_res_norm_kernel, n_cast=n_cast),
        grid=(n_cast + T // tm,),
        in_specs=[pl.BlockSpec((tm, K), row),
                  pl.BlockSpec((None, rc, D), lambda i: (layer, jnp.minimum(i, n_cast - 1), 0)),
                  pl.BlockSpec((tm, D), row),
                  pl.BlockSpec((1, D), lambda i: (0, 0))],
        out_specs=out_specs,
        out_shape=out_shape,
        scratch_shapes=[pltpu.VMEM((K, D), BF16)],
        compiler_params=_params("arbitrary"),
        name="proj_res_norm",
    )(a, w_all, x, g.reshape(1, D))
    return outs if emit_x else outs[0]


def _ffn_in_kernel(h_ref, wa_ref, wb_ref, o_ref, wab_ref, wbb_ref):
    _cache_bf16(wa_ref, wab_ref)
    _cache_bf16(wb_ref, wbb_ref)
    rc = _blk(o_ref.shape[0], DOT_ROWS)
    for r0 in range(0, o_ref.shape[0], rc):
        rows = slice(r0, r0 + rc)
        h = h_ref[rows, :]
        a = _mm(h, wab_ref[...])
        b = _mm(h, wbb_ref[...])
        o_ref[rows, :] = (a * jax.nn.sigmoid(a) * b).astype(o_ref.dtype)


def _ffn_in(h, w_ffn_in, layer):
    T, D = h.shape
    F = w_ffn_in.shape[2] // 2
    tm, tn = _blk(T, 2 * DOT_ROWS), _blk(F, 512)
    nb = F // tn
    return pl.pallas_call(
        _ffn_in_kernel,
        grid=(nb, T // tm),
        in_specs=[pl.BlockSpec((tm, D), lambda j, i: (i, 0)),
                  pl.BlockSpec((None, D, tn), lambda j, i: (layer, 0, j)),
                  pl.BlockSpec((None, D, tn), lambda j, i: (layer, 0, nb + j))],
        out_specs=pl.BlockSpec((tm, tn), lambda j, i: (i, j)),
        out_shape=jax.ShapeDtypeStruct((T, F), BF16),
        scratch_shapes=[pltpu.VMEM((D, tn), BF16)] * 2,
        compiler_params=_params("parallel", "arbitrary"),
        name="ffn_in",
    )(h, w_ffn_in, w_ffn_in)


def kernel(x, positions, norm_attn, w_in, gate_b, lam, subln, sgu_ln_g, sgu_ln_b, w_spatial,
           b_spatial, w_branch, w_out, norm_ffn, w_ffn_in, w_ffn_out, norm_final):
    B, S, D = x.shape
    T = B * S
    depth = w_in.shape[0]
    hd = lam.shape[-1]
    H = D // subln.shape[-1]
    groups, chunk = w_spatial.shape[1], w_spatial.shape[2]
    W = sgu_ln_g.shape[-1]
    assert S % chunk == 0 and w_in.shape[2] == 3 * D + 2 * W + 2 * D

    inv_freq = ROPE_THETA ** (-jnp.arange(0, hd, 2, dtype=F32) / hd)
    invf = jnp.concatenate([inv_freq, inv_freq]).reshape(1, hd)
    cos, sin = _rope_tables(positions.reshape(T, 1), invf)

    xf = x.reshape(T, D)
    h = _rmsnorm(xf, norm_attn[0], BF16)
    for l in range(depth):
        lam_init = 0.8 - 0.6 * math.exp(-0.3 * l)
        qk = _in_proj(h, w_in, l, 0, 2 * D, cos, sin)
        v = _in_proj(h, w_in, l, 2 * D, D)
        o = _diff_attention(qk, v, lam[l], subln[l], lam_init, B, S, H, hd)
        sg = _sgu(h, w_in, l, 3 * D, sgu_ln_g[l], sgu_ln_b[l], w_spatial[l], b_spatial[l].T,
                  chunk, groups)
        merged = _merge(h, o, sg, w_in, l, 3 * D + 2 * W, gate_b[l], w_branch)
        xf, h2 = _proj_res_norm(merged, w_out, l, xf, norm_ffn[l], BF16)

        act = _ffn_in(h2, w_ffn_in, l)
        if l + 1 < depth:
            xf, h = _proj_res_norm(act, w_ffn_out, l, xf, norm_attn[l + 1], BF16)
        else:
            out = _proj_res_norm(act, w_ffn_out, l, xf, norm_final, x.dtype, emit_x=False)
    return out.reshape(B, S, D)
```

```python
import functools
import math

import jax
import jax.numpy as jnp
from jax import lax
from jax.experimental import pallas as pl
from jax.experimental.pallas import tpu as pltpu

ROPE_THETA = 10000.0
EPS = 1e-6
LANES = 128
MXU_COLS = 256
ROW_CHUNK = 256
DOT_ROWS = 1024
VMEM_LIMIT_BYTES = 56 * 1024 * 1024

F32 = jnp.float32
BF16 = jnp.bfloat16


def _blk(dim, pref):
    if dim <= pref:
        return dim
    b = pref - pref % LANES
    while b > LANES and dim % b:
        b -= LANES
    assert dim % b == 0, (dim, pref)
    return b


def _params(*sem):
    return pltpu.CompilerParams(dimension_semantics=sem, vmem_limit_bytes=VMEM_LIMIT_BYTES)


def _mm(a, b):
    return jnp.dot(a, b, preferred_element_type=F32)


def _rope_kernel(pos_ref, invf_ref, cos_ref, sin_ref):
    ang = pos_ref[...].astype(F32) * invf_ref[...]
    lane = lax.broadcasted_iota(jnp.int32, ang.shape, 1)
    sin = jnp.sin(ang)
    cos_ref[...] = jnp.cos(ang)
    sin_ref[...] = jnp.where(lane < ang.shape[1] // 2, -sin, sin)


def _rope_tables(pos, invf):
    T = pos.shape[0]
    hd = invf.shape[1]
    tm = _blk(T, 2048)
    return pl.pallas_call(
        _rope_kernel,
        grid=(T // tm,),
        in_specs=[pl.BlockSpec((tm, 1), lambda i: (i, 0)),
                  pl.BlockSpec((1, hd), lambda i: (0, 0))],
        out_specs=[pl.BlockSpec((tm, hd), lambda i: (i, 0)),
                   pl.BlockSpec((tm, hd), lambda i: (i, 0))],
        out_shape=[jax.ShapeDtypeStruct((T, hd), F32)] * 2,
        compiler_params=_params("parallel"),
        name="rope_tables",
    )(pos, invf)


def _rmsnorm_kernel(x_ref, g_ref, o_ref):
    x = x_ref[...]
    y = x * lax.rsqrt(jnp.mean(x * x, axis=-1, keepdims=True) + EPS)
    o_ref[...] = (y * g_ref[...]).astype(o_ref.dtype)


def _rmsnorm(x, g, out_dtype):
    T, D = x.shape
    tm = _blk(T, 512)
    return pl.pallas_call(
        _rmsnorm_kernel,
        grid=(T // tm,),
        in_specs=[pl.BlockSpec((tm, D), lambda i: (i, 0)),
                  pl.BlockSpec((1, D), lambda i: (0, 0))],
        out_specs=pl.BlockSpec((tm, D), lambda i: (i, 0)),
        out_shape=jax.ShapeDtypeStruct((T, D), out_dtype),
        compiler_params=_params("parallel"),
        name="rmsnorm",
    )(x, g.reshape(1, D))


def _cache_bf16(w_ref, wb_ref):
    @pl.when(pl.program_id(1) == 0)
    def _():
        wb_ref[...] = w_ref[...].astype(wb_ref.dtype)


def _qk_kernel(h_ref, w_ref, cos_ref, sin_ref, o_ref, wb_ref, *, hd):
    _cache_bf16(w_ref, wb_ref)
    rc = _blk(o_ref.shape[0], ROW_CHUNK)
    for r0 in range(0, o_ref.shape[0], rc):
        rows = slice(r0, r0 + rc)
        acc = _mm(h_ref[rows, :], wb_ref[...])
        cos = cos_ref[rows, :]
        sin = sin_ref[rows, :]
        for s0 in range(0, o_ref.shape[1], hd):
            t = acc[:, s0:s0 + hd]
            r = t * cos + pltpu.roll(t, hd // 2, 1) * sin
            o_ref[rows, s0:s0 + hd] = r.astype(o_ref.dtype)


def _v_kernel(h_ref, w_ref, o_ref, wb_ref):
    _cache_bf16(w_ref, wb_ref)
    rc = _blk(o_ref.shape[0], DOT_ROWS)
    for r0 in range(0, o_ref.shape[0], rc):
        rows = slice(r0, r0 + rc)
        o_ref[rows, :] = _mm(h_ref[rows, :], wb_ref[...]).astype(o_ref.dtype)


def _in_proj(h, w_in, layer, col0, n_cols, cos=None, sin=None):
    T, D = h.shape
    tm, tn = _blk(T, 2 * DOT_ROWS), _blk(n_cols, 1024)
    assert col0 % tn == 0
    c0 = col0 // tn
    in_specs = [pl.BlockSpec((tm, D), lambda j, i: (i, 0)),
                pl.BlockSpec((None, D, tn), lambda j, i: (layer, 0, c0 + j))]
    args = [h, w_in]
    if cos is None:
        body = _v_kernel
    else:
        hd = cos.shape[1]
        body = functools.partial(_qk_kernel, hd=hd)
        in_specs += [pl.BlockSpec((tm, hd), lambda j, i: (i, 0))] * 2
        args += [cos, sin]
    return pl.pallas_call(
        body,
        grid=(n_cols // tn, T // tm),
        in_specs=in_specs,
        out_specs=pl.BlockSpec((tm, tn), lambda j, i: (i, j)),
        out_shape=jax.ShapeDtypeStruct((T, n_cols), BF16),
        scratch_shapes=[pltpu.VMEM((D, tn), BF16)],
        compiler_params=_params("parallel", "arbitrary"),
        name="qk_proj" if cos is not None else "v_proj",
    )(*args)


def _attn_kernel(scal_ref, lam_ref, subln_ref, q_ref, k_ref, v_ref, *refs, blk, hd, n_side):
    side_in, o_ref, side_out = refs[:n_side], refs[n_side], refs[n_side + 1:2 * n_side + 1]
    (vt_ref, qt_ref, sa_ref, sb_ref, xa_ref, xb_ref,
     m1_ref, l1_ref, a1_ref, m2_ref, l2_ref, a2_ref) = refs[2 * n_side + 1:]

    for w_ref, wb_ref in zip(side_in, side_out):
        wb_ref[...] = w_ref[...].astype(wb_ref.dtype)

    nq = vt_ref.shape[0]
    unroll = 4 if nq % 4 == 0 else nq
    c = hd ** -0.5 * math.log2(math.e)

    for n in range(nq):
        rows = slice(n * blk, (n + 1) * blk)
        vt_ref[n] = v_ref[rows, :].T
        qt_ref[n, 0] = q_ref[rows, :hd].T
        qt_ref[n, 1] = q_ref[rows, hd:].T

    lp = lam_ref[...]
    lam_init = scal_ref[0]
    lam_full = (jnp.exp(jnp.sum(lp[0:1] * lp[1:2], keepdims=True))
                - jnp.exp(jnp.sum(lp[2:3] * lp[3:4], keepdims=True)) + lam_init)

    def update(s, smax, vt, m_ref, l_ref, a_ref):
        m_old = m_ref[...]
        m_new = jnp.maximum(m_old, smax)
        alpha = jnp.exp2((m_old - m_new) * c)
        p = jnp.exp2((s - m_new) * c)
        l_ref[...] = alpha * l_ref[...] + jnp.sum(p, axis=0, keepdims=True)
        a_ref[...] = alpha * a_ref[...] + _mm(vt, p.astype(vt.dtype))
        m_ref[...] = m_new

    def init_stats():
        for m_ref, l_ref, a_ref in ((m1_ref, l1_ref, a1_ref), (m2_ref, l2_ref, a2_ref)):
            m_ref[...] = jnp.full(m_ref.shape, -jnp.inf, F32)
            l_ref[...] = jnp.zeros(l_ref.shape, F32)
            a_ref[...] = jnp.zeros(a_ref.shape, F32)

    def scores(qi, j, slot):
        buf, mx = slot
        k = k_ref[pl.ds(pl.multiple_of(j * blk, blk), blk), :]
        for i in range(2):
            s = _mm(k[:, i * hd:(i + 1) * hd], qt_ref[qi, i])
            buf[i] = s
            mx[i] = jnp.max(s, axis=0, keepdims=True)

    def process(j, slot, masked):
        buf, mx = slot
        vt = vt_ref[j]
        for i, (m_ref, l_ref, a_ref) in enumerate(((m1_ref, l1_ref, a1_ref),
                                                   (m2_ref, l2_ref, a2_ref))):
            s = buf[i]
            if masked:
                r = lax.broadcasted_iota(jnp.int32, s.shape, 0)
                q = lax.broadcasted_iota(jnp.int32, s.shape, 1)
                s = jnp.where(r <= q, s, -jnp.inf)
                smax = jnp.max(s, axis=0, keepdims=True)
            else:
                smax = mx[i]
            update(s, smax, vt, m_ref, l_ref, a_ref)

    def finish(qi):
        o = a1_ref[...] / l1_ref[...] - lam_full * (a2_ref[...] / l2_ref[...])
        init_stats()
        y = o * lax.rsqrt(jnp.mean(o * o, axis=0, keepdims=True) + EPS)
        y = (y * subln_ref[...]) * (1.0 - lam_init)
        o_ref[pl.ds(pl.multiple_of(qi * blk, blk), blk), :] = y.T.astype(o_ref.dtype)

    def q_block(qi, odd, cur, oth):
        def body(jj, carry):
            j = 2 * jj
            scores(qi, j + 1, oth)
            process(j, cur, masked=False)
            scores(qi, j + 2, cur)
            process(j + 1, oth, masked=False)
            return carry

        lax.fori_loop(0, qi // 2, body, 0)
        nxt = jnp.minimum(qi + 1, nq - 1)
        if odd:
            scores(qi, qi, oth)
            process(qi - 1, cur, masked=False)
            scores(nxt, 0, cur)
            process(qi, oth, masked=True)
            first = cur
        else:
            scores(nxt, 0, oth)
            process(qi, cur, masked=True)
            first = oth
        finish(qi)
        return first

    slot_a, slot_b = (sa_ref, xa_ref), (sb_ref, xb_ref)

    def q_group(g, carry):
        cur = slot_a
        for r in range(unroll):
            oth = slot_b if cur is slot_a else slot_a
            cur = q_block(g * unroll + r, r % 2 == 1, cur, oth)
        assert cur is slot_a or nq == unroll
        return carry

    init_stats()
    scores(0, 0, slot_a)
    if nq == unroll:
        q_group(0, 0)
    else:
        lax.fori_loop(0, nq // unroll, q_group, 0)


def _diff_attention(qk, v, lam_l, subln_l, lam_init, B, S, H, hd, side=(), blk_pref=512):
    vhd = 2 * hd
    blk = _blk(S, blk_pref)
    qk3 = qk.reshape(B, S, 2 * H * vhd)
    v3 = v.reshape(B, S, H * vhd)
    scal = jnp.full((1,), lam_init, F32)
    stat = pltpu.VMEM((1, blk), F32)
    acc = pltpu.VMEM((vhd, blk), F32)
    scores = pltpu.VMEM((2, blk, blk), F32)
    nq = S // blk
    steps = B * H
    side_in, side_out, side_shape = [], [], []
    for arr, lead, colblk, cols in side:
        rows = arr.shape[len(lead)]
        rc = rows // steps
        assert rows % steps == 0 and rc % 16 == 0, (rows, steps)
        assert arr.ndim == len(lead) + 2 and arr.shape[-1] % cols == 0
        side_in.append(pl.BlockSpec((None,) * len(lead) + (rc, cols),
                                    lambda b, h, lead=lead, cb=colblk: lead + (b * H + h, cb)))
        side_out.append(pl.BlockSpec((rc, cols), lambda b, h: (b * H + h, 0)))
        side_shape.append(jax.ShapeDtypeStruct((rows, cols), BF16))
    outs = pl.pallas_call(
        functools.partial(_attn_kernel, blk=blk, hd=hd, n_side=len(side)),
        grid=(B, H),
        in_specs=[pl.BlockSpec(memory_space=pltpu.SMEM),
                  pl.BlockSpec((4, hd), lambda b, h: (0, 0)),
                  pl.BlockSpec((vhd, 1), lambda b, h: (0, 0)),
                  pl.BlockSpec((None, S, vhd), lambda b, h: (b, 0, h)),
                  pl.BlockSpec((None, S, vhd), lambda b, h: (b, 0, H + h)),
                  pl.BlockSpec((None, S, vhd), lambda b, h: (b, 0, h))] + side_in,
        out_specs=[pl.BlockSpec((None, S, vhd), lambda b, h: (b, 0, h))] + side_out,
        out_shape=[jax.ShapeDtypeStruct((B, S, H * vhd), BF16)] + side_shape,
        scratch_shapes=[pltpu.VMEM((nq, vhd, blk), BF16), pltpu.VMEM((nq, 2, hd, blk), BF16),
                        scores, scores, pltpu.VMEM((2, 1, blk), F32), pltpu.VMEM((2, 1, blk), F32),
                        stat, stat, acc, stat, stat, acc],
        compiler_params=_params("parallel", "parallel"),
        name="diff_attention",
    )(scal, lam_l, subln_l.reshape(vhd, 1), qk3, qk3, v3, *[s[0] for s in side])
    return (outs[0].reshape(B * S, H * vhd), *outs[1:])


def _sgu_kernel(h_ref, wub_ref, wvb_ref, lng_ref, lnb_ref, ws_ref, bs_ref, o_ref, u_ref,
                *, chunk, groups):
    W = o_ref.shape[1]
    gd = W // groups
    cw = _blk(W, 512)
    r = lax.broadcasted_iota(jnp.int32, (chunk, chunk), 0)
    c = lax.broadcasted_iota(jnp.int32, (chunk, chunk), 1)
    causal = r >= c
    sub = u_ref.shape[1]

    for sb in range(u_ref.shape[0]):
        b0 = sb * sub
        u_sb = u_ref.at[sb]
        h = h_ref[b0:b0 + sub, :]

        def gelu_cols(wb_ref, c0):
            z = _mm(h, wb_ref[:, c0:c0 + cw])
            return 0.5 * z * (1.0 + lax.erf(z * (2.0 ** -0.5)))

        vg = jnp.concatenate([gelu_cols(wvb_ref, c0) for c0 in range(0, W, cw)], axis=1)
        mu = jnp.mean(vg, axis=-1, keepdims=True)
        d = vg - mu
        var = jnp.mean(d * d, axis=-1, keepdims=True)
        vn = ((d * lax.rsqrt(var + EPS)) * lng_ref[...] + lnb_ref[...]).astype(BF16)
        for c0 in range(0, W, cw):
            u_sb[:, c0:c0 + cw] = gelu_cols(wub_ref, c0)
        for g in range(groups):
            w = jnp.where(causal, ws_ref[g], 0.0).astype(BF16)
            bias = bs_ref[:, g:g + 1]
            cols = slice(g * gd, (g + 1) * gd)
            for n in range(sub // chunk):
                rows = slice(n * chunk, (n + 1) * chunk)
                mixed = _mm(w, vn[rows, cols]) + bias
                o_ref[b0 + n * chunk:b0 + (n + 1) * chunk, cols] = (
                    u_sb[rows, cols] * mixed).astype(o_ref.dtype)


def _resident(shape):
    return pl.BlockSpec(shape, lambda *_: (0,) * len(shape), pipeline_mode=pl.Buffered(1))


def _sgu(h, wu, wv, ln_g, ln_b, ws, bs_t, chunk, groups):
    T, D = h.shape
    W = ln_g.shape[0]
    sub = 2 * chunk if T % (2 * chunk) == 0 else chunk
    tm = 2 * sub if T % (2 * sub) == 0 else sub
    return pl.pallas_call(
        functools.partial(_sgu_kernel, chunk=chunk, groups=groups),
        grid=(T // tm,),
        in_specs=[pl.BlockSpec((tm, D), lambda i: (i, 0)),
                  _resident((D, W)), _resident((D, W)),
                  pl.BlockSpec((1, W), lambda i: (0, 0)),
                  pl.BlockSpec((1, W), lambda i: (0, 0)),
                  pl.BlockSpec((groups, chunk, chunk), lambda i: (0, 0, 0)),
                  pl.BlockSpec((chunk, groups), lambda i: (0, 0))],
        out_specs=pl.BlockSpec((tm, W), lambda i: (i, 0)),
        out_shape=jax.ShapeDtypeStruct((T, W), BF16),
        scratch_shapes=[pltpu.VMEM((tm // sub, sub, W), F32)],
        compiler_params=_params("parallel"),
        name="sgu",
    )(h, wu, wv, ln_g.reshape(1, W), ln_b.reshape(1, W), ws, bs_t)


def _merge_kernel(h_ref, o_ref, sg_ref, wga_ref, wgs_ref, wa_ref, ws_ref, ba_ref, bs_ref, out_ref):
    rc = _blk(out_ref.shape[0], 512)
    for r0 in range(0, out_ref.shape[0], rc):
        rows = slice(r0, r0 + rc)
        h = h_ref[rows, :]
        g_attn = jax.nn.sigmoid(_mm(h, wga_ref[...]) + ba_ref[...])
        g_sgu = jax.nn.sigmoid(_mm(h, wgs_ref[...]) + bs_ref[...])
        y = (g_attn * _mm(o_ref[rows, :], wa_ref[...])
             + g_sgu * _mm(sg_ref[rows, :], ws_ref[...]))
        out_ref[rows, :] = y.astype(out_ref.dtype)


def _merge(h, o, sg, wga, wgs, wa, ws, gate_b):
    T, D = h.shape
    tm, tn = _blk(T, 1024), _blk(D, 512)
    nb = D // tn
    row = lambda j, i: (i, 0)
    col = lambda j, i: (0, j)
    return pl.pallas_call(
        _merge_kernel,
        grid=(nb, T // tm),
        in_specs=[pl.BlockSpec((tm, D), row),
                  pl.BlockSpec((tm, o.shape[1]), row),
                  pl.BlockSpec((tm, sg.shape[1]), row),
                  pl.BlockSpec((D, tn), col),
                  pl.BlockSpec((D, tn), col),
                  pl.BlockSpec((o.shape[1], tn), col),
                  pl.BlockSpec((sg.shape[1], tn), col),
                  pl.BlockSpec((1, tn), col),
                  pl.BlockSpec((1, tn), lambda j, i: (0, nb + j))],
        out_specs=pl.BlockSpec((tm, tn), lambda j, i: (i, j)),
        out_shape=jax.ShapeDtypeStruct((T, D), BF16),
        compiler_params=_params("parallel", "parallel"),
        name="merge",
    )(h, o, sg, wga, wgs, wa, ws, gate_b.reshape(1, -1), gate_b.reshape(1, -1))


def _proj_res_norm_kernel(a_ref, w_ref, x_ref, g_ref, *out_refs):
    rc = _blk(x_ref.shape[0], ROW_CHUNK)
    for r0 in range(0, x_ref.shape[0], rc):
        rows = slice(r0, r0 + rc)
        y = x_ref[rows, :] + _mm(a_ref[rows, :], w_ref[...])
        if len(out_refs) == 2:
            out_refs[0][rows, :] = y
        n = y * lax.rsqrt(jnp.mean(y * y, axis=-1, keepdims=True) + EPS)
        out_refs[-1][rows, :] = (n * g_ref[...]).astype(out_refs[-1].dtype)


def _proj_res_norm(a, w, x, g, norm_dtype, emit_x=True):
    T, K = a.shape
    D = w.shape[1]
    tm = _blk(T, 512 if K <= D else 256)
    row = lambda i: (i, 0)
    out_specs = [pl.BlockSpec((tm, D), row)] * (2 if emit_x else 1)
    out_shape = ([jax.ShapeDtypeStruct((T, D), F32)] if emit_x else []) + [
        jax.ShapeDtypeStruct((T, D), norm_dtype)]
    outs = pl.pallas_call(
        _proj_res_norm_kernel,
        grid=(T // tm,),
        in_specs=[pl.BlockSpec((tm, K), row),
                  _resident((K, D)),
                  pl.BlockSpec((tm, D), row),
                  pl.BlockSpec((1, D), lambda i: (0, 0))],
        out_specs=out_specs,
        out_shape=out_shape,
        compiler_params=_params("parallel"),
        name="proj_res_norm",
    )(a, w, x, g.reshape(1, D))
    return outs if emit_x else outs[0]


def _ffn_in_kernel(h_ref, wa_ref, wb_ref, o_ref, wab_ref, wbb_ref):
    _cache_bf16(wa_ref, wab_ref)
    _cache_bf16(wb_ref, wbb_ref)
    rc = _blk(o_ref.shape[0], DOT_ROWS)
    for r0 in range(0, o_ref.shape[0], rc):
        rows = slice(r0, r0 + rc)
        h = h_ref[rows, :]
        a = _mm(h, wab_ref[...])
        b = _mm(h, wbb_ref[...])
        o_ref[rows, :] = (a * jax.nn.sigmoid(a) * b).astype(o_ref.dtype)


def _ffn_in(h, w_ffn_in, layer):
    T, D = h.shape
    F = w_ffn_in.shape[2] // 2
    tm, tn = _blk(T, 2 * DOT_ROWS), _blk(F, 512)
    nb = F // tn
    return pl.pallas_call(
        _ffn_in_kernel,
        grid=(nb, T // tm),
        in_specs=[pl.BlockSpec((tm, D), lambda j, i: (i, 0)),
                  pl.BlockSpec((None, D, tn), lambda j, i: (layer, 0, j)),
                  pl.BlockSpec((None, D, tn), lambda j, i: (layer, 0, nb + j))],
        out_specs=pl.BlockSpec((tm, tn), lambda j, i: (i, j)),
        out_shape=jax.ShapeDtypeStruct((T, F), BF16),
        scratch_shapes=[pltpu.VMEM((D, tn), BF16)] * 2,
        compiler_params=_params("parallel", "arbitrary"),
        name="ffn_in",
    )(h, w_ffn_in, w_ffn_in)


def kernel(x, positions, norm_attn, w_in, gate_b, lam, subln, sgu_ln_g, sgu_ln_b, w_spatial,
           b_spatial, w_branch, w_out, norm_ffn, w_ffn_in, w_ffn_out, norm_final):
    B, S, D = x.shape
    T = B * S
    depth = w_in.shape[0]
    hd = lam.shape[-1]
    H = D // subln.shape[-1]
    groups, chunk = w_spatial.shape[1], w_spatial.shape[2]
    W = sgu_ln_g.shape[-1]
    assert S % chunk == 0 and w_in.shape[2] == 3 * D + 2 * W + 2 * D

    inv_freq = ROPE_THETA ** (-jnp.arange(0, hd, 2, dtype=F32) / hd)
    invf = jnp.concatenate([inv_freq, inv_freq]).reshape(1, hd)
    cos, sin = _rope_tables(positions.reshape(T, 1), invf)

    xf = x.reshape(T, D)
    h = _rmsnorm(xf, norm_attn[0], BF16)
    for l in range(depth):
        lam_init = 0.8 - 0.6 * math.exp(-0.3 * l)
        qk = _in_proj(h, w_in, l, 0, 2 * D, cos, sin)
        v = _in_proj(h, w_in, l, 2 * D, D)
        side = [(w_in, (l,), (3 * D) // W, W), (w_in, (l,), (3 * D) // W + 1, W),
                (w_in, (l,), (3 * D + 2 * W) // D, D), (w_in, (l,), (3 * D + 2 * W) // D + 1, D),
                (w_branch, (l, 0), 0, D), (w_branch, (l, 1), 0, D),
                (w_out, (l,), 0, D), (w_ffn_out, (l,), 0, D)]
        o, w_zu, w_zv, w_ga, w_gs, w_ba, w_bs, w_o, w_f = _diff_attention(
            qk, v, lam[l], subln[l], lam_init, B, S, H, hd, side)
        sg = _sgu(h, w_zu, w_zv, sgu_ln_g[l], sgu_ln_b[l], w_spatial[l], b_spatial[l].T,
                  chunk, groups)
        merged = _merge(h, o, sg, w_ga, w_gs, w_ba, w_bs, gate_b[l])
        xf, h2 = _proj_res_norm(merged, w_o, xf, norm_ffn[l], BF16)

        act = _ffn_in(h2, w_ffn_in, l)
        if l + 1 < depth:
            xf, h = _proj_res_norm(act, w_f, xf, norm_attn[l + 1], BF16)
        else:
            out = _proj_res_norm(act, w_f, xf, norm_final, x.dtype, emit_x=False)
    return out.reshape(B, S, D)
```

```python
import functools
import math

import jax
import jax.numpy as jnp
from jax import lax
from jax.experimental import pallas as pl
from jax.experimental.pallas import tpu as pltpu

ROPE_THETA = 10000.0
EPS = 1e-6
LANES = 128
MXU_COLS = 256
ROW_CHUNK = 256
DOT_ROWS = 1024
ATTN_BLOCK = 512
VMEM_LIMIT_BYTES = 56 * 1024 * 1024

F32 = jnp.float32
BF16 = jnp.bfloat16


def _blk(dim, pref):
    if dim <= pref:
        return dim
    b = pref - pref % LANES
    while b > LANES and dim % b:
        b -= LANES
    assert dim % b == 0, (dim, pref)
    return b


def _params(*sem):
    return pltpu.CompilerParams(dimension_semantics=sem, vmem_limit_bytes=VMEM_LIMIT_BYTES)


def _mm(a, b):
    return jnp.dot(a, b, preferred_element_type=F32)


def _rope_kernel(pos_ref, invf_ref, cos_ref, sin_ref):
    ang = pos_ref[...].astype(F32) * invf_ref[...]
    lane = lax.broadcasted_iota(jnp.int32, ang.shape, 1)
    sin = jnp.sin(ang)
    cos_ref[...] = jnp.cos(ang)
    sin_ref[...] = jnp.where(lane < ang.shape[1] // 2, -sin, sin)


def _rope_tables(pos, invf):
    T = pos.shape[0]
    hd = invf.shape[1]
    tm = _blk(T, 2048)
    return pl.pallas_call(
        _rope_kernel,
        grid=(T // tm,),
        in_specs=[pl.BlockSpec((tm, 1), lambda i: (i, 0)),
                  pl.BlockSpec((1, hd), lambda i: (0, 0))],
        out_specs=[pl.BlockSpec((tm, hd), lambda i: (i, 0)),
                   pl.BlockSpec((tm, hd), lambda i: (i, 0))],
        out_shape=[jax.ShapeDtypeStruct((T, hd), F32)] * 2,
        compiler_params=_params("parallel"),
        name="rope_tables",
    )(pos, invf)


def _rmsnorm_kernel(x_ref, g_ref, o_ref):
    x = x_ref[...]
    y = x * lax.rsqrt(jnp.mean(x * x, axis=-1, keepdims=True) + EPS)
    o_ref[...] = (y * g_ref[...]).astype(o_ref.dtype)


def _rmsnorm(x, g, out_dtype):
    T, D = x.shape
    tm = _blk(T, 512)
    return pl.pallas_call(
        _rmsnorm_kernel,
        grid=(T // tm,),
        in_specs=[pl.BlockSpec((tm, D), lambda i: (i, 0)),
                  pl.BlockSpec((1, D), lambda i: (0, 0))],
        out_specs=pl.BlockSpec((tm, D), lambda i: (i, 0)),
        out_shape=jax.ShapeDtypeStruct((T, D), out_dtype),
        compiler_params=_params("parallel"),
        name="rmsnorm",
    )(x, g.reshape(1, D))


def _cache_bf16(w_ref, wb_ref):
    @pl.when(pl.program_id(1) == 0)
    def _():
        wb_ref[...] = w_ref[...].astype(wb_ref.dtype)


def _in_proj_kernel(h_ref, w_ref, *refs, hd, rope, tblk):
    if rope:
        cos_ref, sin_ref, o_ref, wb_ref = refs
    else:
        o_ref, wb_ref = refs
    _cache_bf16(w_ref, wb_ref)
    tm, tn = h_ref.shape[0], wb_ref.shape[1]
    rc = _blk(tm, ROW_CHUNK) if tblk is None else min(_blk(tm, ROW_CHUNK), tblk)

    def store(r0, s0, val):
        if tblk is None:
            o_ref[r0:r0 + rc, s0:s0 + val.shape[1]] = val.astype(o_ref.dtype)
        else:
            qb, t0 = divmod(r0, tblk)
            o_ref[qb, s0:s0 + val.shape[1], t0:t0 + rc] = val.astype(o_ref.dtype).T

    for r0 in range(0, tm, rc):
        rows = slice(r0, r0 + rc)
        acc = _mm(h_ref[rows, :], wb_ref[...])
        if not rope:
            store(r0, 0, acc)
            continue
        cos = cos_ref[rows, :]
        sin = sin_ref[rows, :]
        for s0 in range(0, tn, hd):
            t = acc[:, s0:s0 + hd]
            store(r0, s0, t * cos + pltpu.roll(t, hd // 2, 1) * sin)


def _in_proj(h, w_in, layer, col0, n_cols, name, cos=None, sin=None, seq_blk=None):
    T, D = h.shape
    tn = _blk(n_cols, 1024)
    assert col0 % tn == 0
    c0 = col0 // tn
    if seq_blk is None:
        tm, tblk = _blk(T, 2 * DOT_ROWS), None
        out_spec = pl.BlockSpec((tm, tn), lambda j, i: (i, j))
        out_shape = jax.ShapeDtypeStruct((T, n_cols), BF16)
    else:
        S, tblk = seq_blk
        tm = _blk(S, 2 * DOT_ROWS)
        per_seq = S // tm
        out_spec = pl.BlockSpec((None, tm // tblk, tn, tblk),
                                lambda j, i: (i // per_seq, i % per_seq, j, 0))
        out_shape = jax.ShapeDtypeStruct((T // S, S // tblk, n_cols, tblk), BF16)
    in_specs = [pl.BlockSpec((tm, D), lambda j, i: (i, 0)),
                pl.BlockSpec((None, D, tn), lambda j, i: (layer, 0, c0 + j))]
    args = [h, w_in]
    hd = None
    if cos is not None:
        hd = cos.shape[1]
        in_specs += [pl.BlockSpec((tm, hd), lambda j, i: (i, 0))] * 2
        args += [cos, sin]
    return pl.pallas_call(
        functools.partial(_in_proj_kernel, hd=hd, rope=cos is not None, tblk=tblk),
        grid=(n_cols // tn, T // tm),
        in_specs=in_specs,
        out_specs=out_spec,
        out_shape=out_shape,
        scratch_shapes=[pltpu.VMEM((D, tn), BF16)],
        compiler_params=_params("parallel", "arbitrary"),
        name=name,
    )(*args)


def _attn_kernel(scal_ref, lam_ref, subln_ref, qt_ref, k_ref, vt_ref, *refs, blk, hd, n_side):
    side_in, o_ref, side_out = refs[:n_side], refs[n_side], refs[n_side + 1:2 * n_side + 1]
    (sa_ref, sb_ref, xa_ref, xb_ref,
     m1_ref, l1_ref, a1_ref, m2_ref, l2_ref, a2_ref) = refs[2 * n_side + 1:]

    for w_ref, wb_ref in zip(side_in, side_out):
        wb_ref[...] = w_ref[...].astype(wb_ref.dtype)

    nq = vt_ref.shape[0]
    unroll = 4 if nq % 4 == 0 else nq
    c = hd ** -0.5 * math.log2(math.e)

    lp = lam_ref[...]
    lam_init = scal_ref[0]
    lam_full = (jnp.exp(jnp.sum(lp[0:1] * lp[1:2], keepdims=True))
                - jnp.exp(jnp.sum(lp[2:3] * lp[3:4], keepdims=True)) + lam_init)

    def update(s, smax, vt, m_ref, l_ref, a_ref):
        m_old = m_ref[...]
        m_new = jnp.maximum(m_old, smax)
        alpha = jnp.exp2((m_old - m_new) * c)
        p = jnp.exp2((s - m_new) * c)
        l_ref[...] = alpha * l_ref[...] + jnp.sum(p, axis=0, keepdims=True)
        a_ref[...] = alpha * a_ref[...] + _mm(vt, p.astype(vt.dtype))
        m_ref[...] = m_new

    def init_stats():
        for m_ref, l_ref, a_ref in ((m1_ref, l1_ref, a1_ref), (m2_ref, l2_ref, a2_ref)):
            m_ref[...] = jnp.full(m_ref.shape, -jnp.inf, F32)
            l_ref[...] = jnp.zeros(l_ref.shape, F32)
            a_ref[...] = jnp.zeros(a_ref.shape, F32)

    def scores(qi, j, slot):
        buf, mx = slot
        k = k_ref[pl.ds(pl.multiple_of(j * blk, blk), blk), :]
        for i in range(2):
            s = _mm(k[:, i * hd:(i + 1) * hd], qt_ref[qi, i * hd:(i + 1) * hd, :])
            buf[i] = s
            mx[i] = jnp.max(s, axis=0, keepdims=True)

    def process(j, slot, masked):
        buf, mx = slot
        vt = vt_ref[j]
        for i, (m_ref, l_ref, a_ref) in enumerate(((m1_ref, l1_ref, a1_ref),
                                                   (m2_ref, l2_ref, a2_ref))):
            s = buf[i]
            if masked:
                r = lax.broadcasted_iota(jnp.int32, s.shape, 0)
                q = lax.broadcasted_iota(jnp.int32, s.shape, 1)
                s = jnp.where(r <= q, s, -jnp.inf)
                smax = jnp.max(s, axis=0, keepdims=True)
            else:
                smax = mx[i]
            update(s, smax, vt, m_ref, l_ref, a_ref)

    def finish(qi):
        o = a1_ref[...] / l1_ref[...] - lam_full * (a2_ref[...] / l2_ref[...])
        init_stats()
        y = o * lax.rsqrt(jnp.mean(o * o, axis=0, keepdims=True) + EPS)
        y = (y * subln_ref[...]) * (1.0 - lam_init)
        o_ref[pl.ds(pl.multiple_of(qi * blk, blk), blk), :] = y.T.astype(o_ref.dtype)

    def q_block(qi, odd, cur, oth):
        def body(jj, carry):
            j = 2 * jj
            scores(qi, j + 1, oth)
            process(j, cur, masked=False)
            scores(qi, j + 2, cur)
            process(j + 1, oth, masked=False)
            return carry

        lax.fori_loop(0, qi // 2, body, 0)
        nxt = jnp.minimum(qi + 1, nq - 1)
        if odd:
            scores(qi, qi, oth)
            process(qi - 1, cur, masked=False)
            scores(nxt, 0, cur)
            process(qi, oth, masked=True)
            first = cur
        else:
            scores(nxt, 0, oth)
            process(qi, cur, masked=True)
            first = oth
        finish(qi)
        return first

    slot_a, slot_b = (sa_ref, xa_ref), (sb_ref, xb_ref)

    def q_group(g, carry):
        cur = slot_a
        for r in range(unroll):
            oth = slot_b if cur is slot_a else slot_a
            cur = q_block(g * unroll + r, r % 2 == 1, cur, oth)
        assert cur is slot_a or nq == unroll
        return carry

    init_stats()
    scores(0, 0, slot_a)
    if nq == unroll:
        q_group(0, 0)
    else:
        lax.fori_loop(0, nq // unroll, q_group, 0)


def _diff_attention(qt, k, vt, lam_l, subln_l, lam_init, H, hd, side=()):
    vhd = 2 * hd
    B, nq, _, blk = qt.shape
    S = nq * blk
    k3 = k.reshape(B, S, H * vhd)
    scal = jnp.full((1,), lam_init, F32)
    stat = pltpu.VMEM((1, blk), F32)
    acc = pltpu.VMEM((vhd, blk), F32)
    scores = pltpu.VMEM((2, blk, blk), F32)
    tblock = pl.BlockSpec((None, nq, vhd, blk), lambda b, h: (b, 0, h, 0))
    steps = B * H
    side_in, side_out, side_shape = [], [], []
    for arr, lead, colblk, cols in side:
        rows = arr.shape[len(lead)]
        rc = rows // steps
        assert rows % steps == 0 and rc % 16 == 0, (rows, steps)
        assert arr.ndim == len(lead) + 2 and arr.shape[-1] % cols == 0
        side_in.append(pl.BlockSpec((None,) * len(lead) + (rc, cols),
                                    lambda b, h, lead=lead, cb=colblk: lead + (b * H + h, cb)))
        side_out.append(pl.BlockSpec((rc, cols), lambda b, h: (b * H + h, 0)))
        side_shape.append(jax.ShapeDtypeStruct((rows, cols), BF16))
    outs = pl.pallas_call(
        functools.partial(_attn_kernel, blk=blk, hd=hd, n_side=len(side)),
        grid=(B, H),
        in_specs=[pl.BlockSpec(memory_space=pltpu.SMEM),
                  pl.BlockSpec((4, hd), lambda b, h: (0, 0)),
                  pl.BlockSpec((vhd, 1), lambda b, h: (0, 0)),
                  tblock,
                  pl.BlockSpec((None, S, vhd), lambda b, h: (b, 0, h)),
                  tblock] + side_in,
        out_specs=[pl.BlockSpec((None, S, vhd), lambda b, h: (b, 0, h))] + side_out,
        out_shape=[jax.ShapeDtypeStruct((B, S, H * vhd), BF16)] + side_shape,
        scratch_shapes=[scores, scores, pltpu.VMEM((2, 1, blk), F32), pltpu.VMEM((2, 1, blk), F32),
                        stat, stat, acc, stat, stat, acc],
        compiler_params=_params("parallel", "parallel"),
        name="diff_attention",
    )(scal, lam_l, subln_l.reshape(vhd, 1), qt, k3, vt, *[s[0] for s in side])
    return (outs[0].reshape(B * S, H * vhd), *outs[1:])


def _sgu_kernel(h_ref, wub_ref, wvb_ref, lng_ref, lnb_ref, ws_ref, bs_ref, o_ref, u_ref,
                *, chunk, groups):
    W = o_ref.shape[1]
    gd = W // groups
    cw = _blk(W, 512)
    r = lax.broadcasted_iota(jnp.int32, (chunk, chunk), 0)
    c = lax.broadcasted_iota(jnp.int32, (chunk, chunk), 1)
    causal = r >= c
    sub = u_ref.shape[1]

    for sb in range(u_ref.shape[0]):
        b0 = sb * sub
        u_sb = u_ref.at[sb]
        h = h_ref[b0:b0 + sub, :]

        def gelu_cols(wb_ref, c0):
            z = _mm(h, wb_ref[:, c0:c0 + cw])
            return 0.5 * z * (1.0 + lax.erf(z * (2.0 ** -0.5)))

        vg = jnp.concatenate([gelu_cols(wvb_ref, c0) for c0 in range(0, W, cw)], axis=1)
        mu = jnp.mean(vg, axis=-1, keepdims=True)
        d = vg - mu
        var = jnp.mean(d * d, axis=-1, keepdims=True)
        vn = ((d * lax.rsqrt(var + EPS)) * lng_ref[...] + lnb_ref[...]).astype(BF16)
        for c0 in range(0, W, cw):
            u_sb[:, c0:c0 + cw] = gelu_cols(wub_ref, c0)
        for g in range(groups):
            w = jnp.where(causal, ws_ref[g], 0.0).astype(BF16)
            bias = bs_ref[:, g:g + 1]
            cols = slice(g * gd, (g + 1) * gd)
            for n in range(sub // chunk):
                rows = slice(n * chunk, (n + 1) * chunk)
                mixed = _mm(w, vn[rows, cols]) + bias
                o_ref[b0 + n * chunk:b0 + (n + 1) * chunk, cols] = (
                    u_sb[rows, cols] * mixed).astype(o_ref.dtype)


def _resident(shape):
    return pl.BlockSpec(shape, lambda *_: (0,) * len(shape), pipeline_mode=pl.Buffered(1))


def _sgu(h, wu, wv, ln_g, ln_b, ws, bs_t, chunk, groups):
    T, D = h.shape
    W = ln_g.shape[0]
    sub = 2 * chunk if T % (2 * chunk) == 0 else chunk
    tm = 2 * sub if T % (2 * sub) == 0 else sub
    return pl.pallas_call(
        functools.partial(_sgu_kernel, chunk=chunk, groups=groups),
        grid=(T // tm,),
        in_specs=[pl.BlockSpec((tm, D), lambda i: (i, 0)),
                  _resident((D, W)), _resident((D, W)),
                  pl.BlockSpec((1, W), lambda i: (0, 0)),
                  pl.BlockSpec((1, W), lambda i: (0, 0)),
                  pl.BlockSpec((groups, chunk, chunk), lambda i: (0, 0, 0)),
                  pl.BlockSpec((chunk, groups), lambda i: (0, 0))],
        out_specs=pl.BlockSpec((tm, W), lambda i: (i, 0)),
        out_shape=jax.ShapeDtypeStruct((T, W), BF16),
        scratch_shapes=[pltpu.VMEM((tm // sub, sub, W), F32)],
        compiler_params=_params("parallel"),
        name="sgu",
    )(h, wu, wv, ln_g.reshape(1, W), ln_b.reshape(1, W), ws, bs_t)


def _merge_kernel(h_ref, o_ref, sg_ref, wga_ref, wgs_ref, wa_ref, ws_ref, ba_ref, bs_ref, out_ref):
    rc = _blk(out_ref.shape[0], 512)
    for r0 in range(0, out_ref.shape[0], rc):
        rows = slice(r0, r0 + rc)
        h = h_ref[rows, :]
        g_attn = jax.nn.sigmoid(_mm(h, wga_ref[...]) + ba_ref[...])
        g_sgu = jax.nn.sigmoid(_mm(h, wgs_ref[...]) + bs_ref[...])
        y = (g_attn * _mm(o_ref[rows, :], wa_ref[...])
             + g_sgu * _mm(sg_ref[rows, :], ws_ref[...]))
        out_ref[rows, :] = y.astype(out_ref.dtype)


def _merge(h, o, sg, wga, wgs, wa, ws, gate_b):
    T, D = h.shape
    tm, tn = _blk(T, 1024), _blk(D, 512)
    nb = D // tn
    row = lambda j, i: (i, 0)
    col = lambda j, i: (0, j)
    return pl.pallas_call(
        _merge_kernel,
        grid=(nb, T // tm),
        in_specs=[pl.BlockSpec((tm, D), row),
                  pl.BlockSpec((tm, o.shape[1]), row),
                  pl.BlockSpec((tm, sg.shape[1]), row),
                  pl.BlockSpec((D, tn), col),
                  pl.BlockSpec((D, tn), col),
                  pl.BlockSpec((o.shape[1], tn), col),
                  pl.BlockSpec((sg.shape[1], tn), col),
                  pl.BlockSpec((1, tn), col),
                  pl.BlockSpec((1, tn), lambda j, i: (0, nb + j))],
        out_specs=pl.BlockSpec((tm, tn), lambda j, i: (i, j)),
        out_shape=jax.ShapeDtypeStruct((T, D), BF16),
        compiler_params=_params("parallel", "parallel"),
        name="merge",
    )(h, o, sg, wga, wgs, wa, ws, gate_b.reshape(1, -1), gate_b.reshape(1, -1))


def _proj_res_norm_kernel(a_ref, w_ref, x_ref, g_ref, *out_refs):
    rc = _blk(x_ref.shape[0], ROW_CHUNK)
    for r0 in range(0, x_ref.shape[0], rc):
        rows = slice(r0, r0 + rc)
        y = x_ref[rows, :] + _mm(a_ref[rows, :], w_ref[...])
        if len(out_refs) == 2:
            out_refs[0][rows, :] = y
        n = y * lax.rsqrt(jnp.mean(y * y, axis=-1, keepdims=True) + EPS)
        out_refs[-1][rows, :] = (n * g_ref[...]).astype(out_refs[-1].dtype)


def _proj_res_norm(a, w, x, g, norm_dtype, emit_x=True):
    T, K = a.shape
    D = w.shape[1]
    tm = _blk(T, 512 if K <= D else 256)
    row = lambda i: (i, 0)
    out_specs = [pl.BlockSpec((tm, D), row)] * (2 if emit_x else 1)
    out_shape = ([jax.ShapeDtypeStruct((T, D), F32)] if emit_x else []) + [
        jax.ShapeDtypeStruct((T, D), norm_dtype)]
    outs = pl.pallas_call(
        _proj_res_norm_kernel,
        grid=(T // tm,),
        in_specs=[pl.BlockSpec((tm, K), row),
                  _resident((K, D)),
                  pl.BlockSpec((tm, D), row),
                  pl.BlockSpec((1, D), lambda i: (0, 0))],
        out_specs=out_specs,
        out_shape=out_shape,
        compiler_params=_params("parallel"),
        name="proj_res_norm",
    )(a, w, x, g.reshape(1, D))
    return outs if emit_x else outs[0]


def _ffn_in_kernel(h_ref, wa_ref, wb_ref, o_ref, wab_ref, wbb_ref):
    _cache_bf16(wa_ref, wab_ref)
    _cache_bf16(wb_ref, wbb_ref)
    rc = _blk(o_ref.shape[0], DOT_ROWS)
    for r0 in range(0, o_ref.shape[0], rc):
        rows = slice(r0, r0 + rc)
        h = h_ref[rows, :]
        a = _mm(h, wab_ref[...])
        b = _mm(h, wbb_ref[...])
        o_ref[rows, :] = (a * jax.nn.sigmoid(a) * b).astype(o_ref.dtype)


def _ffn_in(h, w_ffn_in, layer):
    T, D = h.shape
    F = w_ffn_in.shape[2] // 2
    tm, tn = _blk(T, 2 * DOT_ROWS), _blk(F, 512)
    nb = F // tn
    return pl.pallas_call(
        _ffn_in_kernel,
        grid=(nb, T // tm),
        in_specs=[pl.BlockSpec((tm, D), lambda j, i: (i, 0)),
                  pl.BlockSpec((None, D, tn), lambda j, i: (layer, 0, j)),
                  pl.BlockSpec((None, D, tn), lambda j, i: (layer, 0, nb + j))],
        out_specs=pl.BlockSpec((tm, tn), lambda j, i: (i, j)),
        out_shape=jax.ShapeDtypeStruct((T, F), BF16),
        scratch_shapes=[pltpu.VMEM((D, tn), BF16)] * 2,
        compiler_params=_params("parallel", "arbitrary"),
        name="ffn_in",
    )(h, w_ffn_in, w_ffn_in)


def kernel(x, positions, norm_attn, w_in, gate_b, lam, subln, sgu_ln_g, sgu_ln_b, w_spatial,
           b_spatial, w_branch, w_out, norm_ffn, w_ffn_in, w_ffn_out, norm_final):
    B, S, D = x.shape
    T = B * S
    depth = w_in.shape[0]
    hd = lam.shape[-1]
    H = D // subln.shape[-1]
    groups, chunk = w_spatial.shape[1], w_spatial.shape[2]
    W = sgu_ln_g.shape[-1]
    assert S % chunk == 0 and w_in.shape[2] == 3 * D + 2 * W + 2 * D
    assert (3 * D) % W == 0 and (2 * W) % D == 0
    blk = _blk(S, ATTN_BLOCK)

    inv_freq = ROPE_THETA ** (-jnp.arange(0, hd, 2, dtype=F32) / hd)
    invf = jnp.concatenate([inv_freq, inv_freq]).reshape(1, hd)
    cos, sin = _rope_tables(positions.reshape(T, 1), invf)

    xf = x.reshape(T, D)
    h = _rmsnorm(xf, norm_attn[0], BF16)
    for l in range(depth):
        lam_init = 0.8 - 0.6 * math.exp(-0.3 * l)
        qt = _in_proj(h, w_in, l, 0, D, "q_proj", cos, sin, seq_blk=(S, blk))
        k = _in_proj(h, w_in, l, D, D, "k_proj", cos, sin)
        vt = _in_proj(h, w_in, l, 2 * D, D, "v_proj", seq_blk=(S, blk))
        side = [(w_in, (l,), (3 * D) // W, W), (w_in, (l,), (3 * D) // W + 1, W),
                (w_in, (l,), (3 * D + 2 * W) // D, D), (w_in, (l,), (3 * D + 2 * W) // D + 1, D),
                (w_branch, (l, 0), 0, D), (w_branch, (l, 1), 0, D),
                (w_out, (l,), 0, D), (w_ffn_out, (l,), 0, D)]
        o, w_zu, w_zv, w_ga, w_gs, w_ba, w_bs, w_o, w_f = _diff_attention(
            qt, k, vt, lam[l], subln[l], lam_init, H, hd, side)
        sg = _sgu(h, w_zu, w_zv, sgu_ln_g[l], sgu_ln_b[l], w_spatial[l], b_spatial[l].T,
                  chunk, groups)
        merged = _merge(h, o, sg, w_ga, w_gs, w_ba, w_bs, gate_b[l])
        xf, h2 = _proj_res_norm(merged, w_o, xf, norm_ffn[l], BF16)

        act = _ffn_in(h2, w_ffn_in, l)
        if l + 1 < depth:
            xf, h = _proj_res_norm(act, w_f, xf, norm_attn[l + 1], BF16)
        else:
            out = _proj_res_norm(act, w_f, xf, norm_final, x.dtype, emit_x=False)
    return out.reshape(B, S, D)
```

```python
import functools
import math

import jax
import jax.numpy as jnp
from jax import lax
from jax.experimental import pallas as pl
from jax.experimental.pallas import tpu as pltpu

ROPE_THETA = 10000.0
EPS = 1e-6
LANES = 128
MXU_COLS = 256
ROW_CHUNK = 256
DOT_ROWS = 1024
VMEM_LIMIT_BYTES = 56 * 1024 * 1024

F32 = jnp.float32
BF16 = jnp.bfloat16


def _blk(dim, pref):
    if dim <= pref:
        return dim
    b = pref - pref % LANES
    while b > LANES and dim % b:
        b -= LANES
    assert dim % b == 0, (dim, pref)
    return b


def _params(*sem):
    return pltpu.CompilerParams(dimension_semantics=sem, vmem_limit_bytes=VMEM_LIMIT_BYTES)


def _mm(a, b):
    return jnp.dot(a, b, preferred_element_type=F32)


def _rope_kernel(pos_ref, invf_ref, cos_ref, sin_ref):
    ang = pos_ref[...].astype(F32) * invf_ref[...]
    lane = lax.broadcasted_iota(jnp.int32, ang.shape, 1)
    sin = jnp.sin(ang)
    cos_ref[...] = jnp.cos(ang)
    sin_ref[...] = jnp.where(lane < ang.shape[1] // 2, -sin, sin)


def _rope_tables(pos, invf):
    T = pos.shape[0]
    hd = invf.shape[1]
    tm = _blk(T, 2048)
    return pl.pallas_call(
        _rope_kernel,
        grid=(T // tm,),
        in_specs=[pl.BlockSpec((tm, 1), lambda i: (i, 0)),
                  pl.BlockSpec((1, hd), lambda i: (0, 0))],
        out_specs=[pl.BlockSpec((tm, hd), lambda i: (i, 0)),
                   pl.BlockSpec((tm, hd), lambda i: (i, 0))],
        out_shape=[jax.ShapeDtypeStruct((T, hd), F32)] * 2,
        compiler_params=_params("parallel"),
        name="rope_tables",
    )(pos, invf)


def _rmsnorm_kernel(x_ref, g_ref, o_ref):
    x = x_ref[...]
    y = x * lax.rsqrt(jnp.mean(x * x, axis=-1, keepdims=True) + EPS)
    o_ref[...] = (y * g_ref[...]).astype(o_ref.dtype)


def _rmsnorm(x, g, out_dtype):
    T, D = x.shape
    tm = _blk(T, 512)
    return pl.pallas_call(
        _rmsnorm_kernel,
        grid=(T // tm,),
        in_specs=[pl.BlockSpec((tm, D), lambda i: (i, 0)),
                  pl.BlockSpec((1, D), lambda i: (0, 0))],
        out_specs=pl.BlockSpec((tm, D), lambda i: (i, 0)),
        out_shape=jax.ShapeDtypeStruct((T, D), out_dtype),
        compiler_params=_params("parallel"),
        name="rmsnorm",
    )(x, g.reshape(1, D))


def _cache_bf16(w_ref, wb_ref):
    @pl.when(pl.program_id(1) == 0)
    def _():
        wb_ref[...] = w_ref[...].astype(wb_ref.dtype)


def _qk_kernel(h_ref, w_ref, cos_ref, sin_ref, o_ref, wb_ref, *, hd):
    _cache_bf16(w_ref, wb_ref)
    tm, vhd = o_ref.shape[1], o_ref.shape[2]
    rc = _blk(tm, ROW_CHUNK)
    for r0 in range(0, tm, rc):
        rows = slice(r0, r0 + rc)
        acc = _mm(h_ref[rows, :], wb_ref[...])
        cos = cos_ref[rows, :]
        sin = sin_ref[rows, :]
        for s0 in range(0, wb_ref.shape[1], hd):
            t = acc[:, s0:s0 + hd]
            r = t * cos + pltpu.roll(t, hd // 2, 1) * sin
            o_ref[s0 // vhd, rows, s0 % vhd:s0 % vhd + hd] = r.astype(o_ref.dtype)


def _v_kernel(h_ref, w_ref, o_ref, wb_ref):
    _cache_bf16(w_ref, wb_ref)
    tm, vhd = o_ref.shape[1], o_ref.shape[2]
    rc = _blk(tm, DOT_ROWS)
    for r0 in range(0, tm, rc):
        rows = slice(r0, r0 + rc)
        acc = _mm(h_ref[rows, :], wb_ref[...])
        for hh in range(o_ref.shape[0]):
            o_ref[hh, rows, :] = acc[:, hh * vhd:(hh + 1) * vhd].astype(o_ref.dtype)


def _in_proj(h, w_in, layer, col0, n_cols, S, vhd, cos=None, sin=None):
    T, D = h.shape
    tm, tn = _blk(S, 2 * DOT_ROWS), _blk(n_cols, 1024)
    assert col0 % tn == 0 and tn % vhd == 0
    c0 = col0 // tn
    per_seq = S // tm
    in_specs = [pl.BlockSpec((tm, D), lambda j, i: (i, 0)),
                pl.BlockSpec((None, D, tn), lambda j, i: (layer, 0, c0 + j))]
    args = [h, w_in]
    if cos is None:
        body = _v_kernel
    else:
        hd = cos.shape[1]
        body = functools.partial(_qk_kernel, hd=hd)
        in_specs += [pl.BlockSpec((tm, hd), lambda j, i: (i, 0))] * 2
        args += [cos, sin]
    return pl.pallas_call(
        body,
        grid=(n_cols // tn, T // tm),
        in_specs=in_specs,
        out_specs=pl.BlockSpec((None, tn // vhd, tm, vhd),
                               lambda j, i: (i // per_seq, j, i % per_seq, 0)),
        out_shape=jax.ShapeDtypeStruct((T // S, n_cols // vhd, S, vhd), BF16),
        scratch_shapes=[pltpu.VMEM((D, tn), BF16)],
        compiler_params=_params("parallel", "arbitrary"),
        name="qk_proj" if cos is not None else "v_proj",
    )(*args)


def _attn_kernel(scal_ref, lam_ref, subln_ref, q_ref, k_ref, v_ref, *refs, blk, hd, n_side):
    side_in, o_ref, side_out = refs[:n_side], refs[n_side], refs[n_side + 1:2 * n_side + 1]
    (vt_ref, qt_ref, sa_ref, sb_ref, xa_ref, xb_ref,
     m1_ref, l1_ref, a1_ref, m2_ref, l2_ref, a2_ref) = refs[2 * n_side + 1:]

    for w_ref, wb_ref in zip(side_in, side_out):
        wb_ref[...] = w_ref[...].astype(wb_ref.dtype)

    nq = vt_ref.shape[0]
    unroll = 4 if nq % 4 == 0 else nq
    c = hd ** -0.5 * math.log2(math.e)

    for n in range(nq):
        rows = slice(n * blk, (n + 1) * blk)
        vt_ref[n] = v_ref[rows, :].T
        qt_ref[n, 0] = q_ref[rows, :hd].T
        qt_ref[n, 1] = q_ref[rows, hd:].T

    lp = lam_ref[...]
    lam_init = scal_ref[0]
    lam_full = (jnp.exp(jnp.sum(lp[0:1] * lp[1:2], keepdims=True))
                - jnp.exp(jnp.sum(lp[2:3] * lp[3:4], keepdims=True)) + lam_init)

    def update(s, smax, vt, m_ref, l_ref, a_ref):
        m_old = m_ref[...]
        m_new = jnp.maximum(m_old, smax)
        alpha = jnp.exp2((m_old - m_new) * c)
        p = jnp.exp2((s - m_new) * c)
        l_ref[...] = alpha * l_ref[...] + jnp.sum(p, axis=0, keepdims=True)
        a_ref[...] = alpha * a_ref[...] + _mm(vt, p.astype(vt.dtype))
        m_ref[...] = m_new

    def init_stats():
        for m_ref, l_ref, a_ref in ((m1_ref, l1_ref, a1_ref), (m2_ref, l2_ref, a2_ref)):
            m_ref[...] = jnp.full(m_ref.shape, -jnp.inf, F32)
            l_ref[...] = jnp.zeros(l_ref.shape, F32)
            a_ref[...] = jnp.zeros(a_ref.shape, F32)

    def scores(qi, j, slot):
        buf, mx = slot
        k = k_ref[pl.ds(pl.multiple_of(j * blk, blk), blk), :]
        for i in range(2):
            s = _mm(k[:, i * hd:(i + 1) * hd], qt_ref[qi, i])
            buf[i] = s
            mx[i] = jnp.max(s, axis=0, keepdims=True)

    def process(j, slot, masked):
        buf, mx = slot
        vt = vt_ref[j]
        for i, (m_ref, l_ref, a_ref) in enumerate(((m1_ref, l1_ref, a1_ref),
                                                   (m2_ref, l2_ref, a2_ref))):
            s = buf[i]
            if masked:
                r = lax.broadcasted_iota(jnp.int32, s.shape, 0)
                q = lax.broadcasted_iota(jnp.int32, s.shape, 1)
                s = jnp.where(r <= q, s, -jnp.inf)
                smax = jnp.max(s, axis=0, keepdims=True)
            else:
                smax = mx[i]
            update(s, smax, vt, m_ref, l_ref, a_ref)

    def finish(qi):
        o = a1_ref[...] / l1_ref[...] - lam_full * (a2_ref[...] / l2_ref[...])
        init_stats()
        y = o * lax.rsqrt(jnp.mean(o * o, axis=0, keepdims=True) + EPS)
        y = (y * subln_ref[...]) * (1.0 - lam_init)
        o_ref[pl.ds(pl.multiple_of(qi * blk, blk), blk), :] = y.T.astype(o_ref.dtype)

    def q_block(qi, odd, cur, oth):
        def body(jj, carry):
            j = 2 * jj
            scores(qi, j + 1, oth)
            process(j, cur, masked=False)
            scores(qi, j + 2, cur)
            process(j + 1, oth, masked=False)
            return carry

        lax.fori_loop(0, qi // 2, body, 0)
        nxt = jnp.minimum(qi + 1, nq - 1)
        if odd:
            scores(qi, qi, oth)
            process(qi - 1, cur, masked=False)
            scores(nxt, 0, cur)
            process(qi, oth, masked=True)
            first = cur
        else:
            scores(nxt, 0, oth)
            process(qi, cur, masked=True)
            first = oth
        finish(qi)
        return first

    slot_a, slot_b = (sa_ref, xa_ref), (sb_ref, xb_ref)

    def q_group(g, carry):
        cur = slot_a
        for r in range(unroll):
            oth = slot_b if cur is slot_a else slot_a
            cur = q_block(g * unroll + r, r % 2 == 1, cur, oth)
        assert cur is slot_a or nq == unroll
        return carry

    init_stats()
    scores(0, 0, slot_a)
    if nq == unroll:
        q_group(0, 0)
    else:
        lax.fori_loop(0, nq // unroll, q_group, 0)


def _diff_attention(qk, v, lam_l, subln_l, lam_init, B, S, H, hd, side=(), blk_pref=512):
    vhd = 2 * hd
    blk = _blk(S, blk_pref)
    head = lambda h0: pl.BlockSpec((None, None, S, vhd), lambda b, h: (b, h0 + h, 0, 0))
    scal = jnp.full((1,), lam_init, F32)
    stat = pltpu.VMEM((1, blk), F32)
    acc = pltpu.VMEM((vhd, blk), F32)
    scores = pltpu.VMEM((2, blk, blk), F32)
    nq = S // blk
    steps = B * H
    side_in, side_out, side_shape = [], [], []
    for arr, lead, colblk, cols in side:
        rows = arr.shape[len(lead)]
        rc = rows // steps
        assert rows % steps == 0 and rc % 16 == 0, (rows, steps)
        assert arr.ndim == len(lead) + 2 and arr.shape[-1] % cols == 0
        side_in.append(pl.BlockSpec((None,) * len(lead) + (rc, cols),
                                    lambda b, h, lead=lead, cb=colblk: lead + (b * H + h, cb)))
        side_out.append(pl.BlockSpec((rc, cols), lambda b, h: (b * H + h, 0)))
        side_shape.append(jax.ShapeDtypeStruct((rows, cols), BF16))
    outs = pl.pallas_call(
        functools.partial(_attn_kernel, blk=blk, hd=hd, n_side=len(side)),
        grid=(B, H),
        in_specs=[pl.BlockSpec(memory_space=pltpu.SMEM),
                  pl.BlockSpec((4, hd), lambda b, h: (0, 0)),
                  pl.BlockSpec((vhd, 1), lambda b, h: (0, 0)),
                  head(0), head(H), head(0)] + side_in,
        out_specs=[head(0)] + side_out,
        out_shape=[jax.ShapeDtypeStruct((B, H, S, vhd), BF16)] + side_shape,
        scratch_shapes=[pltpu.VMEM((nq, vhd, blk), BF16), pltpu.VMEM((nq, 2, hd, blk), BF16),
                        scores, scores, pltpu.VMEM((2, 1, blk), F32), pltpu.VMEM((2, 1, blk), F32),
                        stat, stat, acc, stat, stat, acc],
        compiler_params=_params("parallel", "parallel"),
        name="diff_attention",
    )(scal, lam_l, subln_l.reshape(vhd, 1), qk, qk, v, *[s[0] for s in side])
    return outs


def _sgu_kernel(h_ref, wub_ref, wvb_ref, lng_ref, lnb_ref, ws_ref, bs_ref, o_ref, u_ref,
                *, chunk, groups):
    W = o_ref.shape[1]
    gd = W // groups
    cw = _blk(W, 512)
    r = lax.broadcasted_iota(jnp.int32, (chunk, chunk), 0)
    c = lax.broadcasted_iota(jnp.int32, (chunk, chunk), 1)
    causal = r >= c
    sub = u_ref.shape[1]

    for sb in range(u_ref.shape[0]):
        b0 = sb * sub
        u_sb = u_ref.at[sb]
        h = h_ref[b0:b0 + sub, :]

        def gelu_cols(wb_ref, c0):
            z = _mm(h, wb_ref[:, c0:c0 + cw])
            return 0.5 * z * (1.0 + lax.erf(z * (2.0 ** -0.5)))

        vg = jnp.concatenate([gelu_cols(wvb_ref, c0) for c0 in range(0, W, cw)], axis=1)
        mu = jnp.mean(vg, axis=-1, keepdims=True)
        d = vg - mu
        var = jnp.mean(d * d, axis=-1, keepdims=True)
        vn = ((d * lax.rsqrt(var + EPS)) * lng_ref[...] + lnb_ref[...]).astype(BF16)
        for c0 in range(0, W, cw):
            u_sb[:, c0:c0 + cw] = gelu_cols(wub_ref, c0)
        for g in range(groups):
            w = jnp.where(causal, ws_ref[g], 0.0).astype(BF16)
            bias = bs_ref[:, g:g + 1]
            cols = slice(g * gd, (g + 1) * gd)
            for n in range(sub // chunk):
                rows = slice(n * chunk, (n + 1) * chunk)
                mixed = _mm(w, vn[rows, cols]) + bias
                o_ref[b0 + n * chunk:b0 + (n + 1) * chunk, cols] = (
                    u_sb[rows, cols] * mixed).astype(o_ref.dtype)


def _resident(shape):
    return pl.BlockSpec(shape, lambda *_: (0,) * len(shape), pipeline_mode=pl.Buffered(1))


def _sgu(h, wu, wv, ln_g, ln_b, ws, bs_t, chunk, groups):
    T, D = h.shape
    W = ln_g.shape[0]
    sub = 2 * chunk if T % (2 * chunk) == 0 else chunk
    tm = 2 * sub if T % (2 * sub) == 0 else sub
    return pl.pallas_call(
        functools.partial(_sgu_kernel, chunk=chunk, groups=groups),
        grid=(T // tm,),
        in_specs=[pl.BlockSpec((tm, D), lambda i: (i, 0)),
                  _resident((D, W)), _resident((D, W)),
                  pl.BlockSpec((1, W), lambda i: (0, 0)),
                  pl.BlockSpec((1, W), lambda i: (0, 0)),
                  pl.BlockSpec((groups, chunk, chunk), lambda i: (0, 0, 0)),
                  pl.BlockSpec((chunk, groups), lambda i: (0, 0))],
        out_specs=pl.BlockSpec((tm, W), lambda i: (i, 0)),
        out_shape=jax.ShapeDtypeStruct((T, W), BF16),
        scratch_shapes=[pltpu.VMEM((tm // sub, sub, W), F32)],
        compiler_params=_params("parallel"),
        name="sgu",
    )(h, wu, wv, ln_g.reshape(1, W), ln_b.reshape(1, W), ws, bs_t)


def _merge_kernel(h_ref, o_ref, sg_ref, wga_ref, wgs_ref, wa_ref, ws_ref, ba_ref, bs_ref, out_ref):
    rc = _blk(out_ref.shape[0], 512)
    for r0 in range(0, out_ref.shape[0], rc):
        rows = slice(r0, r0 + rc)
        h = h_ref[rows, :]
        g_attn = jax.nn.sigmoid(_mm(h, wga_ref[...]) + ba_ref[...])
        g_sgu = jax.nn.sigmoid(_mm(h, wgs_ref[...]) + bs_ref[...])
        vhd = o_ref.shape[2]
        y_attn = _mm(o_ref[0, rows, :], wa_ref[:vhd, :])
        for hh in range(1, o_ref.shape[0]):
            y_attn = y_attn + _mm(o_ref[hh, rows, :], wa_ref[hh * vhd:(hh + 1) * vhd, :])
        y = g_attn * y_attn + g_sgu * _mm(sg_ref[rows, :], ws_ref[...])
        out_ref[rows, :] = y.astype(out_ref.dtype)


def _merge(h, o, sg, wga, wgs, wa, ws, gate_b):
    T, D = h.shape
    B, H, S, vhd = o.shape
    tm, tn = _blk(S, 1024), _blk(D, 512)
    nb = D // tn
    per_seq = S // tm
    row = lambda j, i: (i, 0)
    col = lambda j, i: (0, j)
    return pl.pallas_call(
        _merge_kernel,
        grid=(nb, T // tm),
        in_specs=[pl.BlockSpec((tm, D), row),
                  pl.BlockSpec((None, H, tm, vhd), lambda j, i: (i // per_seq, 0, i % per_seq, 0)),
                  pl.BlockSpec((tm, sg.shape[1]), row),
                  pl.BlockSpec((D, tn), col),
                  pl.BlockSpec((D, tn), col),
                  pl.BlockSpec((H * vhd, tn), col),
                  pl.BlockSpec((sg.shape[1], tn), col),
                  pl.BlockSpec((1, tn), col),
                  pl.BlockSpec((1, tn), lambda j, i: (0, nb + j))],
        out_specs=pl.BlockSpec((tm, tn), lambda j, i: (i, j)),
        out_shape=jax.ShapeDtypeStruct((T, D), BF16),
        compiler_params=_params("parallel", "parallel"),
        name="merge",
    )(h, o, sg, wga, wgs, wa, ws, gate_b.reshape(1, -1), gate_b.reshape(1, -1))


def _proj_res_norm_kernel(a_ref, w_ref, x_ref, g_ref, *out_refs):
    rc = _blk(x_ref.shape[0], ROW_CHUNK)
    for r0 in range(0, x_ref.shape[0], rc):
        rows = slice(r0, r0 + rc)
        y = x_ref[rows, :] + _mm(a_ref[rows, :], w_ref[...])
        if len(out_refs) == 2:
            out_refs[0][rows, :] = y
        n = y * lax.rsqrt(jnp.mean(y * y, axis=-1, keepdims=True) + EPS)
        out_refs[-1][rows, :] = (n * g_ref[...]).astype(out_refs[-1].dtype)


def _proj_res_norm(a, w, x, g, norm_dtype, emit_x=True):
    T, K = a.shape
    D = w.shape[1]
    tm = _blk(T, 512 if K <= D else 256)
    row = lambda i: (i, 0)
    out_specs = [pl.BlockSpec((tm, D), row)] * (2 if emit_x else 1)
    out_shape = ([jax.ShapeDtypeStruct((T, D), F32)] if emit_x else []) + [
        jax.ShapeDtypeStruct((T, D), norm_dtype)]
    outs = pl.pallas_call(
        _proj_res_norm_kernel,
        grid=(T // tm,),
        in_specs=[pl.BlockSpec((tm, K), row),
                  _resident((K, D)),
                  pl.BlockSpec((tm, D), row),
                  pl.BlockSpec((1, D), lambda i: (0, 0))],
        out_specs=out_specs,
        out_shape=out_shape,
        compiler_params=_params("parallel"),
        name="proj_res_norm",
    )(a, w, x, g.reshape(1, D))
    return outs if emit_x else outs[0]


def _ffn_in_kernel(h_ref, wa_ref, wb_ref, o_ref, wab_ref, wbb_ref):
    _cache_bf16(wa_ref, wab_ref)
    _cache_bf16(wb_ref, wbb_ref)
    rc = _blk(o_ref.shape[0], DOT_ROWS)
    for r0 in range(0, o_ref.shape[0], rc):
        rows = slice(r0, r0 + rc)
        h = h_ref[rows, :]
        a = _mm(h, wab_ref[...])
        b = _mm(h, wbb_ref[...])
        o_ref[rows, :] = (a * jax.nn.sigmoid(a) * b).astype(o_ref.dtype)


def _ffn_in(h, w_ffn_in, layer):
    T, D = h.shape
    F = w_ffn_in.shape[2] // 2
    tm, tn = _blk(T, 2 * DOT_ROWS), _blk(F, 512)
    nb = F // tn
    return pl.pallas_call(
        _ffn_in_kernel,
        grid=(nb, T // tm),
        in_specs=[pl.BlockSpec((tm, D), lambda j, i: (i, 0)),
                  pl.BlockSpec((None, D, tn), lambda j, i: (layer, 0, j)),
                  pl.BlockSpec((None, D, tn), lambda j, i: (layer, 0, nb + j))],
        out_specs=pl.BlockSpec((tm, tn), lambda j, i: (i, j)),
        out_shape=jax.ShapeDtypeStruct((T, F), BF16),
        scratch_shapes=[pltpu.VMEM((D, tn), BF16)] * 2,
        compiler_params=_params("parallel", "arbitrary"),
        name="ffn_in",
    )(h, w_ffn_in, w_ffn_in)


def kernel(x, positions, norm_attn, w_in, gate_b, lam, subln, sgu_ln_g, sgu_ln_b, w_spatial,
           b_spatial, w_branch, w_out, norm_ffn, w_ffn_in, w_ffn_out, norm_final):
    B, S, D = x.shape
    T = B * S
    depth = w_in.shape[0]
    hd = lam.shape[-1]
    H = D // subln.shape[-1]
    groups, chunk = w_spatial.shape[1], w_spatial.shape[2]
    W = sgu_ln_g.shape[-1]
    assert S % chunk == 0 and w_in.shape[2] == 3 * D + 2 * W + 2 * D

    inv_freq = ROPE_THETA ** (-jnp.arange(0, hd, 2, dtype=F32) / hd)
    invf = jnp.concatenate([inv_freq, inv_freq]).reshape(1, hd)
    cos, sin = _rope_tables(positions.reshape(T, 1), invf)

    xf = x.reshape(T, D)
    h = _rmsnorm(xf, norm_attn[0], BF16)
    for l in range(depth):
        lam_init = 0.8 - 0.6 * math.exp(-0.3 * l)
        qk = _in_proj(h, w_in, l, 0, 2 * D, S, 2 * hd, cos, sin)
        v = _in_proj(h, w_in, l, 2 * D, D, S, 2 * hd)
        side = [(w_in, (l,), (3 * D) // W, W), (w_in, (l,), (3 * D) // W + 1, W),
                (w_in, (l,), (3 * D + 2 * W) // D, D), (w_in, (l,), (3 * D + 2 * W) // D + 1, D),
                (w_branch, (l, 0), 0, D), (w_branch, (l, 1), 0, D),
                (w_out, (l,), 0, D), (w_ffn_out, (l,), 0, D)]
        o, w_zu, w_zv, w_ga, w_gs, w_ba, w_bs, w_o, w_f = _diff_attention(
            qk, v, lam[l], subln[l], lam_init, B, S, H, hd, side)
        sg = _sgu(h, w_zu, w_zv, sgu_ln_g[l], sgu_ln_b[l], w_spatial[l], b_spatial[l].T,
                  chunk, groups)
        merged = _merge(h, o, sg, w_ga, w_gs, w_ba, w_bs, gate_b[l])
        xf, h2 = _proj_res_norm(merged, w_o, xf, norm_ffn[l], BF16)

        act = _ffn_in(h2, w_ffn_in, l)
        if l + 1 < depth:
            xf, h = _proj_res_norm(act, w_f, xf, norm_attn[l + 1], BF16)
        else:
            out = _proj_res_norm(act, w_f, xf, norm_final, x.dtype, emit_x=False)
    return out.reshape(B, S, D)
```

```python
import functools
import math

import jax
import jax.numpy as jnp
from jax import lax
from jax.experimental import pallas as pl
from jax.experimental.pallas import tpu as pltpu

ROPE_THETA = 10000.0
EPS = 1e-6
LANES = 128
MXU_COLS = 256
ROW_CHUNK = 256
DOT_ROWS = 1024
VMEM_LIMIT_BYTES = 56 * 1024 * 1024

F32 = jnp.float32
BF16 = jnp.bfloat16


def _blk(dim, pref):
    if dim <= pref:
        return dim
    b = pref - pref % LANES
    while b > LANES and dim % b:
        b -= LANES
    assert dim % b == 0, (dim, pref)
    return b


def _params(*sem):
    return pltpu.CompilerParams(dimension_semantics=sem, vmem_limit_bytes=VMEM_LIMIT_BYTES)


def _mm(a, b):
    return jnp.dot(a, b, preferred_element_type=F32)


def _rope_kernel(pos_ref, invf_ref, cos_ref, sin_ref):
    ang = pos_ref[...].astype(F32) * invf_ref[...]
    lane = lax.broadcasted_iota(jnp.int32, ang.shape, 1)
    sin = jnp.sin(ang)
    cos_ref[...] = jnp.cos(ang)
    sin_ref[...] = jnp.where(lane < ang.shape[1] // 2, -sin, sin)


def _rope_tables(pos, invf):
    T = pos.shape[0]
    hd = invf.shape[1]
    tm = _blk(T, 2048)
    return pl.pallas_call(
        _rope_kernel,
        grid=(T // tm,),
        in_specs=[pl.BlockSpec((tm, 1), lambda i: (i, 0)),
                  pl.BlockSpec((1, hd), lambda i: (0, 0))],
        out_specs=[pl.BlockSpec((tm, hd), lambda i: (i, 0)),
                   pl.BlockSpec((tm, hd), lambda i: (i, 0))],
        out_shape=[jax.ShapeDtypeStruct((T, hd), F32)] * 2,
        compiler_params=_params("parallel"),
        name="rope_tables",
    )(pos, invf)


def _rmsnorm_kernel(x_ref, g_ref, o_ref):
    x = x_ref[...]
    y = x * lax.rsqrt(jnp.mean(x * x, axis=-1, keepdims=True) + EPS)
    o_ref[...] = (y * g_ref[...]).astype(o_ref.dtype)


def _rmsnorm(x, g, out_dtype):
    T, D = x.shape
    tm = _blk(T, 512)
    return pl.pallas_call(
        _rmsnorm_kernel,
        grid=(T // tm,),
        in_specs=[pl.BlockSpec((tm, D), lambda i: (i, 0)),
                  pl.BlockSpec((1, D), lambda i: (0, 0))],
        out_specs=pl.BlockSpec((tm, D), lambda i: (i, 0)),
        out_shape=jax.ShapeDtypeStruct((T, D), out_dtype),
        compiler_params=_params("parallel"),
        name="rmsnorm",
    )(x, g.reshape(1, D))


def _cache_bf16(w_ref, wb_ref):
    @pl.when(pl.program_id(1) == 0)
    def _():
        wb_ref[...] = w_ref[...].astype(wb_ref.dtype)


def _qk_kernel(h_ref, w_ref, cos_ref, sin_ref, o_ref, wb_ref, *, hd, n_rope):
    _cache_bf16(w_ref, wb_ref)
    rope = pl.program_id(0) < n_rope
    rc = _blk(o_ref.shape[0], ROW_CHUNK)
    for r0 in range(0, o_ref.shape[0], rc):
        rows = slice(r0, r0 + rc)
        acc = _mm(h_ref[rows, :], wb_ref[...])
        cos = jnp.where(rope, cos_ref[rows, :], 1.0)
        sin = jnp.where(rope, sin_ref[rows, :], 0.0)
        for s0 in range(0, o_ref.shape[1], hd):
            t = acc[:, s0:s0 + hd]
            r = t * cos + pltpu.roll(t, hd // 2, 1) * sin
            o_ref[rows, s0:s0 + hd] = r.astype(o_ref.dtype)


def _v_kernel(h_ref, w_ref, o_ref, wb_ref):
    _cache_bf16(w_ref, wb_ref)
    rc = _blk(o_ref.shape[0], DOT_ROWS)
    for r0 in range(0, o_ref.shape[0], rc):
        rows = slice(r0, r0 + rc)
        o_ref[rows, :] = _mm(h_ref[rows, :], wb_ref[...]).astype(o_ref.dtype)


def _in_proj(h, w_in, layer, col0, n_cols, cos=None, sin=None, n_rope_cols=0):
    T, D = h.shape
    tm, tn = _blk(T, 2 * DOT_ROWS), _blk(n_cols, 1024)
    assert col0 % tn == 0
    c0 = col0 // tn
    in_specs = [pl.BlockSpec((tm, D), lambda j, i: (i, 0)),
                pl.BlockSpec((None, D, tn), lambda j, i: (layer, 0, c0 + j))]
    args = [h, w_in]
    if cos is None:
        body = _v_kernel
    else:
        hd = cos.shape[1]
        body = functools.partial(_qk_kernel, hd=hd, n_rope=n_rope_cols // tn)
        in_specs += [pl.BlockSpec((tm, hd), lambda j, i: (i, 0))] * 2
        args += [cos, sin]
    return pl.pallas_call(
        body,
        grid=(n_cols // tn, T // tm),
        in_specs=in_specs,
        out_specs=pl.BlockSpec((tm, tn), lambda j, i: (i, j)),
        out_shape=jax.ShapeDtypeStruct((T, n_cols), BF16),
        scratch_shapes=[pltpu.VMEM((D, tn), BF16)],
        compiler_params=_params("parallel", "arbitrary"),
        name="qkv_proj" if cos is not None else "v_proj",
    )(*args)


def _attn_kernel(scal_ref, lam_ref, subln_ref, q_ref, k_ref, v_ref, *refs, blk, hd, n_side):
    side_in, o_ref, side_out = refs[:n_side], refs[n_side], refs[n_side + 1:2 * n_side + 1]
    (vt_ref, qt_ref, sa_ref, sb_ref, xa_ref, xb_ref,
     m1_ref, l1_ref, a1_ref, m2_ref, l2_ref, a2_ref) = refs[2 * n_side + 1:]

    for w_ref, wb_ref in zip(side_in, side_out):
        wb_ref[...] = w_ref[...].astype(wb_ref.dtype)

    nq = vt_ref.shape[0]
    unroll = 4 if nq % 4 == 0 else nq
    c = hd ** -0.5 * math.log2(math.e)

    for n in range(nq):
        rows = slice(n * blk, (n + 1) * blk)
        vt_ref[n] = v_ref[rows, :].T
        qt_ref[n, 0] = q_ref[rows, :hd].T
        qt_ref[n, 1] = q_ref[rows, hd:].T

    lp = lam_ref[...]
    lam_init = scal_ref[0]
    lam_full = (jnp.exp(jnp.sum(lp[0:1] * lp[1:2], keepdims=True))
                - jnp.exp(jnp.sum(lp[2:3] * lp[3:4], keepdims=True)) + lam_init)

    def update(s, smax, vt, m_ref, l_ref, a_ref):
        m_old = m_ref[...]
        m_new = jnp.maximum(m_old, smax)
        alpha = jnp.exp2((m_old - m_new) * c)
        p = jnp.exp2((s - m_new) * c)
        l_ref[...] = alpha * l_ref[...] + jnp.sum(p, axis=0, keepdims=True)
        a_ref[...] = alpha * a_ref[...] + _mm(vt, p.astype(vt.dtype))
        m_ref[...] = m_new

    def init_stats():
        for m_ref, l_ref, a_ref in ((m1_ref, l1_ref, a1_ref), (m2_ref, l2_ref, a2_ref)):
            m_ref[...] = jnp.full(m_ref.shape, -jnp.inf, F32)
            l_ref[...] = jnp.zeros(l_ref.shape, F32)
            a_ref[...] = jnp.zeros(a_ref.shape, F32)

    def scores(qi, j, slot):
        buf, mx = slot
        k = k_ref[pl.ds(pl.multiple_of(j * blk, blk), blk), :]
        for i in range(2):
            s = _mm(k[:, i * hd:(i + 1) * hd], qt_ref[qi, i])
            buf[i] = s
            mx[i] = jnp.max(s, axis=0, keepdims=True)

    def process(j, slot, masked):
        buf, mx = slot
        vt = vt_ref[j]
        for i, (m_ref, l_ref, a_ref) in enumerate(((m1_ref, l1_ref, a1_ref),
                                                   (m2_ref, l2_ref, a2_ref))):
            s = buf[i]
            if masked:
                r = lax.broadcasted_iota(jnp.int32, s.shape, 0)
                q = lax.broadcasted_iota(jnp.int32, s.shape, 1)
                s = jnp.where(r <= q, s, -jnp.inf)
                smax = jnp.max(s, axis=0, keepdims=True)
            else:
                smax = mx[i]
            update(s, smax, vt, m_ref, l_ref, a_ref)

    def finish(qi):
        o = a1_ref[...] / l1_ref[...] - lam_full * (a2_ref[...] / l2_ref[...])
        init_stats()
        y = o * lax.rsqrt(jnp.mean(o * o, axis=0, keepdims=True) + EPS)
        y = (y * subln_ref[...]) * (1.0 - lam_init)
        o_ref[pl.ds(pl.multiple_of(qi * blk, blk), blk), :] = y.T.astype(o_ref.dtype)

    def q_block(qi, odd, cur, oth):
        def body(jj, carry):
            j = 2 * jj
            scores(qi, j + 1, oth)
            process(j, cur, masked=False)
            scores(qi, j + 2, cur)
            process(j + 1, oth, masked=False)
            return carry

        lax.fori_loop(0, qi // 2, body, 0)
        nxt = jnp.minimum(qi + 1, nq - 1)
        if odd:
            scores(qi, qi, oth)
            process(qi - 1, cur, masked=False)
            scores(nxt, 0, cur)
            process(qi, oth, masked=True)
            first = cur
        else:
            scores(nxt, 0, oth)
            process(qi, cur, masked=True)
            first = oth
        finish(qi)
        return first

    slot_a, slot_b = (sa_ref, xa_ref), (sb_ref, xb_ref)

    def q_group(g, carry):
        cur = slot_a
        for r in range(unroll):
            oth = slot_b if cur is slot_a else slot_a
            cur = q_block(g * unroll + r, r % 2 == 1, cur, oth)
        assert cur is slot_a or nq == unroll
        return carry

    init_stats()
    scores(0, 0, slot_a)
    if nq == unroll:
        q_group(0, 0)
    else:
        lax.fori_loop(0, nq // unroll, q_group, 0)


def _diff_attention(qkv, lam_l, subln_l, lam_init, B, S, H, hd, side=(), blk_pref=512):
    vhd = 2 * hd
    blk = _blk(S, blk_pref)
    qkv3 = qkv.reshape(B, S, 3 * H * vhd)
    scal = jnp.full((1,), lam_init, F32)
    stat = pltpu.VMEM((1, blk), F32)
    acc = pltpu.VMEM((vhd, blk), F32)
    scores = pltpu.VMEM((2, blk, blk), F32)
    nq = S // blk
    steps = B * H
    side_in, side_out, side_shape = [], [], []
    for arr, lead, colblk, cols in side:
        rows = arr.shape[len(lead)]
        rc = rows // steps
        assert rows % steps == 0 and rc % 16 == 0, (rows, steps)
        assert arr.ndim == len(lead) + 2 and arr.shape[-1] % cols == 0
        side_in.append(pl.BlockSpec((None,) * len(lead) + (rc, cols),
                                    lambda b, h, lead=lead, cb=colblk: lead + (b * H + h, cb)))
        side_out.append(pl.BlockSpec((rc, cols), lambda b, h: (b * H + h, 0)))
        side_shape.append(jax.ShapeDtypeStruct((rows, cols), BF16))
    outs = pl.pallas_call(
        functools.partial(_attn_kernel, blk=blk, hd=hd, n_side=len(side)),
        grid=(B, H),
        in_specs=[pl.BlockSpec(memory_space=pltpu.SMEM),
                  pl.BlockSpec((4, hd), lambda b, h: (0, 0)),
                  pl.BlockSpec((vhd, 1), lambda b, h: (0, 0)),
                  pl.BlockSpec((None, S, vhd), lambda b, h: (b, 0, h)),
                  pl.BlockSpec((None, S, vhd), lambda b, h: (b, 0, H + h)),
                  pl.BlockSpec((None, S, vhd), lambda b, h: (b, 0, 2 * H + h))] + side_in,
        out_specs=[pl.BlockSpec((None, S, vhd), lambda b, h: (b, 0, h))] + side_out,
        out_shape=[jax.ShapeDtypeStruct((B, S, H * vhd), BF16)] + side_shape,
        scratch_shapes=[pltpu.VMEM((nq, vhd, blk), BF16), pltpu.VMEM((nq, 2, hd, blk), BF16),
                        scores, scores, pltpu.VMEM((2, 1, blk), F32), pltpu.VMEM((2, 1, blk), F32),
                        stat, stat, acc, stat, stat, acc],
        compiler_params=_params("parallel", "parallel"),
        name="diff_attention",
    )(scal, lam_l, subln_l.reshape(vhd, 1), qkv3, qkv3, qkv3, *[s[0] for s in side])
    return (outs[0].reshape(B * S, H * vhd), *outs[1:])


def _sgu_kernel(h_ref, wub_ref, wvb_ref, lng_ref, lnb_ref, ws_ref, bs_ref, o_ref, u_ref,
                *, chunk, groups):
    W = o_ref.shape[1]
    gd = W // groups
    cw = _blk(W, 512)
    r = lax.broadcasted_iota(jnp.int32, (chunk, chunk), 0)
    c = lax.broadcasted_iota(jnp.int32, (chunk, chunk), 1)
    causal = r >= c
    sub = u_ref.shape[1]

    for sb in range(u_ref.shape[0]):
        b0 = sb * sub
        u_sb = u_ref.at[sb]
        h = h_ref[b0:b0 + sub, :]

        def gelu_cols(wb_ref, c0):
            z = _mm(h, wb_ref[:, c0:c0 + cw])
            return 0.5 * z * (1.0 + lax.erf(z * (2.0 ** -0.5)))

        vg = jnp.concatenate([gelu_cols(wvb_ref, c0) for c0 in range(0, W, cw)], axis=1)
        mu = jnp.mean(vg, axis=-1, keepdims=True)
        d = vg - mu
        var = jnp.mean(d * d, axis=-1, keepdims=True)
        vn = ((d * lax.rsqrt(var + EPS)) * lng_ref[...] + lnb_ref[...]).astype(BF16)
        for c0 in range(0, W, cw):
            u_sb[:, c0:c0 + cw] = gelu_cols(wub_ref, c0)
        for g in range(groups):
            w = jnp.where(causal, ws_ref[g], 0.0).astype(BF16)
            bias = bs_ref[:, g:g + 1]
            cols = slice(g * gd, (g + 1) * gd)
            for n in range(sub // chunk):
                rows = slice(n * chunk, (n + 1) * chunk)
                mixed = _mm(w, vn[rows, cols]) + bias
                o_ref[b0 + n * chunk:b0 + (n + 1) * chunk, cols] = (
                    u_sb[rows, cols] * mixed).astype(o_ref.dtype)


def _resident(shape):
    return pl.BlockSpec(shape, lambda *_: (0,) * len(shape), pipeline_mode=pl.Buffered(1))


def _sgu(h, wu, wv, ln_g, ln_b, ws, bs_t, chunk, groups):
    T, D = h.shape
    W = ln_g.shape[0]
    sub = 2 * chunk if T % (2 * chunk) == 0 else chunk
    tm = 2 * sub if T % (2 * sub) == 0 else sub
    return pl.pallas_call(
        functools.partial(_sgu_kernel, chunk=chunk, groups=groups),
        grid=(T // tm,),
        in_specs=[pl.BlockSpec((tm, D), lambda i: (i, 0)),
                  _resident((D, W)), _resident((D, W)),
                  pl.BlockSpec((1, W), lambda i: (0, 0)),
                  pl.BlockSpec((1, W), lambda i: (0, 0)),
                  pl.BlockSpec((groups, chunk, chunk), lambda i: (0, 0, 0)),
                  pl.BlockSpec((chunk, groups), lambda i: (0, 0))],
        out_specs=pl.BlockSpec((tm, W), lambda i: (i, 0)),
        out_shape=jax.ShapeDtypeStruct((T, W), BF16),
        scratch_shapes=[pltpu.VMEM((tm // sub, sub, W), F32)],
        compiler_params=_params("parallel"),
        name="sgu",
    )(h, wu, wv, ln_g.reshape(1, W), ln_b.reshape(1, W), ws, bs_t)


def _merge_kernel(h_ref, o_ref, sg_ref, wga_ref, wgs_ref, wa_ref, ws_ref, ba_ref, bs_ref, out_ref):
    rc = _blk(out_ref.shape[0], 512)
    for r0 in range(0, out_ref.shape[0], rc):
        rows = slice(r0, r0 + rc)
        h = h_ref[rows, :]
        g_attn = jax.nn.sigmoid(_mm(h, wga_ref[...]) + ba_ref[...])
        g_sgu = jax.nn.sigmoid(_mm(h, wgs_ref[...]) + bs_ref[...])
        y = (g_attn * _mm(o_ref[rows, :], wa_ref[...])
             + g_sgu * _mm(sg_ref[rows, :], ws_ref[...]))
        out_ref[rows, :] = y.astype(out_ref.dtype)


def _merge(h, o, sg, wga, wgs, wa, ws, gate_b):
    T, D = h.shape
    tm, tn = _blk(T, 1024), _blk(D, 512)
    nb = D // tn
    row = lambda j, i: (i, 0)
    col = lambda j, i: (0, j)
    return pl.pallas_call(
        _merge_kernel,
        grid=(nb, T // tm),
        in_specs=[pl.BlockSpec((tm, D), row),
                  pl.BlockSpec((tm, o.shape[1]), row),
                  pl.BlockSpec((tm, sg.shape[1]), row),
                  pl.BlockSpec((D, tn), col),
                  pl.BlockSpec((D, tn), col),
                  pl.BlockSpec((o.shape[1], tn), col),
                  pl.BlockSpec((sg.shape[1], tn), col),
                  pl.BlockSpec((1, tn), col),
                  pl.BlockSpec((1, tn), lambda j, i: (0, nb + j))],
        out_specs=pl.BlockSpec((tm, tn), lambda j, i: (i, j)),
        out_shape=jax.ShapeDtypeStruct((T, D), BF16),
        compiler_params=_params("parallel", "parallel"),
        name="merge",
    )(h, o, sg, wga, wgs, wa, ws, gate_b.reshape(1, -1), gate_b.reshape(1, -1))


def _proj_res_norm_kernel(a_ref, w_ref, x_ref, g_ref, *out_refs):
    rc = _blk(x_ref.shape[0], ROW_CHUNK)
    for r0 in range(0, x_ref.shape[0], rc):
        rows = slice(r0, r0 + rc)
        y = x_ref[rows, :] + _mm(a_ref[rows, :], w_ref[...])
        if len(out_refs) == 2:
            out_refs[0][rows, :] = y
        n = y * lax.rsqrt(jnp.mean(y * y, axis=-1, keepdims=True) + EPS)
        out_refs[-1][rows, :] = (n * g_ref[...]).astype(out_refs[-1].dtype)


def _proj_res_norm(a, w, x, g, norm_dtype, emit_x=True):
    T, K = a.shape
    D = w.shape[1]
    tm = _blk(T, 512 if K <= D else 256)
    row = lambda i: (i, 0)
    out_specs = [pl.BlockSpec((tm, D), row)] * (2 if emit_x else 1)
    out_shape = ([jax.ShapeDtypeStruct((T, D), F32)] if emit_x else []) + [
        jax.ShapeDtypeStruct((T, D), norm_dtype)]
    outs = pl.pallas_call(
        _proj_res_norm_kernel,
        grid=(T // tm,),
        in_specs=[pl.BlockSpec((tm, K), row),
                  _resident((K, D)),
                  pl.BlockSpec((tm, D), row),
                  pl.BlockSpec((1, D), lambda i: (0, 0))],
        out_specs=out_specs,
        out_shape=out_shape,
        compiler_params=_params("parallel"),
        name="proj_res_norm",
    )(a, w, x, g.reshape(1, D))
    return outs if emit_x else outs[0]


def _ffn_in_kernel(h_ref, wa_ref, wb_ref, o_ref, wab_ref, wbb_ref):
    _cache_bf16(wa_ref, wab_ref)
    _cache_bf16(wb_ref, wbb_ref)
    rc = _blk(o_ref.shape[0], DOT_ROWS)
    for r0 in range(0, o_ref.shape[0], rc):
        rows = slice(r0, r0 + rc)
        h = h_ref[rows, :]
        a = _mm(h, wab_ref[...])
        b = _mm(h, wbb_ref[...])
        o_ref[rows, :] = (a * jax.nn.sigmoid(a) * b).astype(o_ref.dtype)


def _ffn_in(h, w_ffn_in, layer):
    T, D = h.shape
    F = w_ffn_in.shape[2] // 2
    tm, tn = _blk(T, 2 * DOT_ROWS), _blk(F, 512)
    nb = F // tn
    return pl.pallas_call(
        _ffn_in_kernel,
        grid=(nb, T // tm),
        in_specs=[pl.BlockSpec((tm, D), lambda j, i: (i, 0)),
                  pl.BlockSpec((None, D, tn), lambda j, i: (layer, 0, j)),
                  pl.BlockSpec((None, D, tn), lambda j, i: (layer, 0, nb + j))],
        out_specs=pl.BlockSpec((tm, tn), lambda j, i: (i, j)),
        out_shape=jax.ShapeDtypeStruct((T, F), BF16),
        scratch_shapes=[pltpu.VMEM((D, tn), BF16)] * 2,
        compiler_params=_params("parallel", "arbitrary"),
        name="ffn_in",
    )(h, w_ffn_in, w_ffn_in)


def kernel(x, positions, norm_attn, w_in, gate_b, lam, subln, sgu_ln_g, sgu_ln_b, w_spatial,
           b_spatial, w_branch, w_out, norm_ffn, w_ffn_in, w_ffn_out, norm_final):
    B, S, D = x.shape
    T = B * S
    depth = w_in.shape[0]
    hd = lam.shape[-1]
    H = D // subln.shape[-1]
    groups, chunk = w_spatial.shape[1], w_spatial.shape[2]
    W = sgu_ln_g.shape[-1]
    assert S % chunk == 0 and w_in.shape[2] == 3 * D + 2 * W + 2 * D

    inv_freq = ROPE_THETA ** (-jnp.arange(0, hd, 2, dtype=F32) / hd)
    invf = jnp.concatenate([inv_freq, inv_freq]).reshape(1, hd)
    cos, sin = _rope_tables(positions.reshape(T, 1), invf)

    xf = x.reshape(T, D)
    h = _rmsnorm(xf, norm_attn[0], BF16)
    for l in range(depth):
        lam_init = 0.8 - 0.6 * math.exp(-0.3 * l)
        qkv = _in_proj(h, w_in, l, 0, 3 * D, cos, sin, n_rope_cols=2 * D)
        side = [(w_in, (l,), (3 * D) // W, W), (w_in, (l,), (3 * D) // W + 1, W),
                (w_in, (l,), (3 * D + 2 * W) // D, D), (w_in, (l,), (3 * D + 2 * W) // D + 1, D),
                (w_branch, (l, 0), 0, D), (w_branch, (l, 1), 0, D),
                (w_out, (l,), 0, D), (w_ffn_out, (l,), 0, D)]
        o, w_zu, w_zv, w_ga, w_gs, w_ba, w_bs, w_o, w_f = _diff_attention(
            qkv, lam[l], subln[l], lam_init, B, S, H, hd, side)
        sg = _sgu(h, w_zu, w_zv, sgu_ln_g[l], sgu_ln_b[l], w_spatial[l], b_spatial[l].T,
                  chunk, groups)
        merged = _merge(h, o, sg, w_ga, w_gs, w_ba, w_bs, gate_b[l])
        xf, h2 = _proj_res_norm(merged, w_o, xf, norm_ffn[l], BF16)

        act = _ffn_in(h2, w_ffn_in, l)
        if l + 1 < depth:
            xf, h = _proj_res_norm(act, w_f, xf, norm_attn[l + 1], BF16)
        else:
            out = _proj_res_norm(act, w_f, xf, norm_final, x.dtype, emit_x=False)
    return out.reshape(B, S, D)
```

```python
import functools
import math

import jax
import jax.numpy as jnp
from jax import lax
from jax.experimental import pallas as pl
from jax.experimental.pallas import tpu as pltpu

ROPE_THETA = 10000.0
EPS = 1e-6
LANES = 128
MXU_COLS = 256
ROW_CHUNK = 256
DOT_ROWS = 1024
VMEM_LIMIT_BYTES = 60 * 1024 * 1024

F32 = jnp.float32
BF16 = jnp.bfloat16


def _blk(dim, pref):
    if dim <= pref:
        return dim
    b = pref - pref % LANES
    while b > LANES and dim % b:
        b -= LANES
    assert dim % b == 0, (dim, pref)
    return b


def _params(*sem):
    return pltpu.CompilerParams(dimension_semantics=sem, vmem_limit_bytes=VMEM_LIMIT_BYTES)


def _mm(a, b):
    return jnp.dot(a, b, preferred_element_type=F32)


def _rope_kernel(pos_ref, invf_ref, cos_ref, sin_ref):
    ang = pos_ref[...].astype(F32) * invf_ref[...]
    lane = lax.broadcasted_iota(jnp.int32, ang.shape, 1)
    sin = jnp.sin(ang)
    cos_ref[...] = jnp.cos(ang)
    sin_ref[...] = jnp.where(lane < ang.shape[1] // 2, -sin, sin)


def _rope_tables(pos, invf):
    T = pos.shape[0]
    hd = invf.shape[1]
    tm = _blk(T, 2048)
    return pl.pallas_call(
        _rope_kernel,
        grid=(T // tm,),
        in_specs=[pl.BlockSpec((tm, 1), lambda i: (i, 0)),
                  pl.BlockSpec((1, hd), lambda i: (0, 0))],
        out_specs=[pl.BlockSpec((tm, hd), lambda i: (i, 0)),
                   pl.BlockSpec((tm, hd), lambda i: (i, 0))],
        out_shape=[jax.ShapeDtypeStruct((T, hd), F32)] * 2,
        compiler_params=_params("parallel"),
        name="rope_tables",
    )(pos, invf)


def _rmsnorm_kernel(x_ref, g_ref, o_ref):
    x = x_ref[...]
    y = x * lax.rsqrt(jnp.mean(x * x, axis=-1, keepdims=True) + EPS)
    o_ref[...] = (y * g_ref[...]).astype(o_ref.dtype)


def _rmsnorm(x, g, out_dtype):
    T, D = x.shape
    tm = _blk(T, 512)
    return pl.pallas_call(
        _rmsnorm_kernel,
        grid=(T // tm,),
        in_specs=[pl.BlockSpec((tm, D), lambda i: (i, 0)),
                  pl.BlockSpec((1, D), lambda i: (0, 0))],
        out_specs=pl.BlockSpec((tm, D), lambda i: (i, 0)),
        out_shape=jax.ShapeDtypeStruct((T, D), out_dtype),
        compiler_params=_params("parallel"),
        name="rmsnorm",
    )(x, g.reshape(1, D))


def _cache_bf16(w_ref, wb_ref):
    @pl.when(pl.program_id(1) == 0)
    def _():
        wb_ref[...] = w_ref[...].astype(wb_ref.dtype)


def _qk_kernel(h_ref, w_ref, cos_ref, sin_ref, o_ref, wb_ref, *, hd, n_rope):
    _cache_bf16(w_ref, wb_ref)
    rope = pl.program_id(0) < n_rope
    rc = _blk(o_ref.shape[0], ROW_CHUNK)
    for r0 in range(0, o_ref.shape[0], rc):
        rows = slice(r0, r0 + rc)
        acc = _mm(h_ref[rows, :], wb_ref[...])
        cos = jnp.where(rope, cos_ref[rows, :], 1.0)
        sin = jnp.where(rope, sin_ref[rows, :], 0.0)
        for s0 in range(0, o_ref.shape[1], hd):
            t = acc[:, s0:s0 + hd]
            r = t * cos + pltpu.roll(t, hd // 2, 1) * sin
            o_ref[rows, s0:s0 + hd] = r.astype(o_ref.dtype)


def _v_kernel(h_ref, w_ref, o_ref, wb_ref):
    _cache_bf16(w_ref, wb_ref)
    rc = _blk(o_ref.shape[0], DOT_ROWS)
    for r0 in range(0, o_ref.shape[0], rc):
        rows = slice(r0, r0 + rc)
        o_ref[rows, :] = _mm(h_ref[rows, :], wb_ref[...]).astype(o_ref.dtype)


def _in_proj(h, w_in, layer, col0, n_cols, cos=None, sin=None, n_rope_cols=0):
    T, D = h.shape
    tm, tn = _blk(T, 2 * DOT_ROWS), _blk(n_cols, 1024)
    assert col0 % tn == 0
    c0 = col0 // tn
    in_specs = [pl.BlockSpec((tm, D), lambda j, i: (i, 0)),
                pl.BlockSpec((None, D, tn), lambda j, i: (layer, 0, c0 + j))]
    args = [h, w_in]
    if cos is None:
        body = _v_kernel
    else:
        hd = cos.shape[1]
        body = functools.partial(_qk_kernel, hd=hd, n_rope=n_rope_cols // tn)
        in_specs += [pl.BlockSpec((tm, hd), lambda j, i: (i, 0))] * 2
        args += [cos, sin]
    return pl.pallas_call(
        body,
        grid=(n_cols // tn, T // tm),
        in_specs=in_specs,
        out_specs=pl.BlockSpec((tm, tn), lambda j, i: (i, j)),
        out_shape=jax.ShapeDtypeStruct((T, n_cols), BF16),
        scratch_shapes=[pltpu.VMEM((D, tn), BF16)],
        compiler_params=_params("parallel", "arbitrary"),
        name="qkv_proj" if cos is not None else "v_proj",
    )(*args)


def _attn_kernel(scal_ref, lam_ref, subln_ref, q_ref, k_ref, v_ref, *refs, blk, hd, n_side):
    side_in, o_ref, side_out = refs[:n_side], refs[n_side], refs[n_side + 1:2 * n_side + 1]
    (vt_ref, qt_ref, sa_ref, sb_ref, xa_ref, xb_ref,
     m1_ref, l1_ref, a1_ref, m2_ref, l2_ref, a2_ref) = refs[2 * n_side + 1:]

    for w_ref, wb_ref in zip(side_in, side_out):
        wb_ref[...] = w_ref[...].astype(wb_ref.dtype)

    nq = vt_ref.shape[0]
    unroll = 4 if nq % 4 == 0 else nq
    c = hd ** -0.5 * math.log2(math.e)

    for n in range(nq):
        rows = slice(n * blk, (n + 1) * blk)
        vt_ref[n] = v_ref[rows, :].T
        qt_ref[n, 0] = q_ref[rows, :hd].T
        qt_ref[n, 1] = q_ref[rows, hd:].T

    lp = lam_ref[...]
    lam_init = scal_ref[0]
    lam_full = (jnp.exp(jnp.sum(lp[0:1] * lp[1:2], keepdims=True))
                - jnp.exp(jnp.sum(lp[2:3] * lp[3:4], keepdims=True)) + lam_init)

    def update(s, smax, vt, m_ref, l_ref, a_ref):
        m_old = m_ref[...]
        m_new = jnp.maximum(m_old, smax)
        alpha = jnp.exp2((m_old - m_new) * c)
        p = jnp.exp2((s - m_new) * c)
        l_ref[...] = alpha * l_ref[...] + jnp.sum(p, axis=0, keepdims=True)
        a_ref[...] = alpha * a_ref[...] + _mm(vt, p.astype(vt.dtype))
        m_ref[...] = m_new

    def init_stats():
        for m_ref, l_ref, a_ref in ((m1_ref, l1_ref, a1_ref), (m2_ref, l2_ref, a2_ref)):
            m_ref[...] = jnp.full(m_ref.shape, -jnp.inf, F32)
            l_ref[...] = jnp.zeros(l_ref.shape, F32)
            a_ref[...] = jnp.zeros(a_ref.shape, F32)

    def scores(qi, j, slot):
        buf, mx = slot
        k = k_ref[pl.ds(pl.multiple_of(j * blk, blk), blk), :]
        for i in range(2):
            s = _mm(k[:, i * hd:(i + 1) * hd], qt_ref[qi, i])
            buf[i] = s
            mx[i] = jnp.max(s, axis=0, keepdims=True)

    def process(j, slot, masked):
        buf, mx = slot
        vt = vt_ref[j]
        for i, (m_ref, l_ref, a_ref) in enumerate(((m1_ref, l1_ref, a1_ref),
                                                   (m2_ref, l2_ref, a2_ref))):
            s = buf[i]
            if masked:
                r = lax.broadcasted_iota(jnp.int32, s.shape, 0)
                q = lax.broadcasted_iota(jnp.int32, s.shape, 1)
                s = jnp.where(r <= q, s, -jnp.inf)
                smax = jnp.max(s, axis=0, keepdims=True)
            else:
                smax = mx[i]
            update(s, smax, vt, m_ref, l_ref, a_ref)

    def finish(qi):
        o = a1_ref[...] / l1_ref[...] - lam_full * (a2_ref[...] / l2_ref[...])
        init_stats()
        y = o * lax.rsqrt(jnp.mean(o * o, axis=0, keepdims=True) + EPS)
        y = (y * subln_ref[...]) * (1.0 - lam_init)
        o_ref[pl.ds(pl.multiple_of(qi * blk, blk), blk), :] = y.T.astype(o_ref.dtype)

    def q_block(qi, odd, cur, oth):
        def body(jj, carry):
            j = 2 * jj
            scores(qi, j + 1, oth)
            process(j, cur, masked=False)
            scores(qi, j + 2, cur)
            process(j + 1, oth, masked=False)
            return carry

        lax.fori_loop(0, qi // 2, body, 0)
        nxt = jnp.minimum(qi + 1, nq - 1)
        if odd:
            scores(qi, qi, oth)
            process(qi - 1, cur, masked=False)
            scores(nxt, 0, cur)
            process(qi, oth, masked=True)
            first = cur
        else:
            scores(nxt, 0, oth)
            process(qi, cur, masked=True)
            first = oth
        finish(qi)
        return first

    slot_a, slot_b = (sa_ref, xa_ref), (sb_ref, xb_ref)

    def q_group(g, carry):
        cur = slot_a
        for r in range(unroll):
            oth = slot_b if cur is slot_a else slot_a
            cur = q_block(g * unroll + r, r % 2 == 1, cur, oth)
        assert cur is slot_a or nq == unroll
        return carry

    init_stats()
    scores(0, 0, slot_a)
    if nq == unroll:
        q_group(0, 0)
    else:
        lax.fori_loop(0, nq // unroll, q_group, 0)


def _diff_attention(qkv, lam_l, subln_l, lam_init, B, S, H, hd, side=(), blk_pref=512):
    vhd = 2 * hd
    blk = _blk(S, blk_pref)
    qkv3 = qkv.reshape(B, S, 3 * H * vhd)
    scal = jnp.full((1,), lam_init, F32)
    stat = pltpu.VMEM((1, blk), F32)
    acc = pltpu.VMEM((vhd, blk), F32)
    scores = pltpu.VMEM((2, blk, blk), F32)
    nq = S // blk
    steps = B * H
    side_in, side_out, side_shape = [], [], []
    for arr, lead, colblk, cols in side:
        rows = arr.shape[len(lead)]
        rc = rows // steps
        assert rows % steps == 0 and rc % 16 == 0, (rows, steps)
        assert arr.ndim == len(lead) + 2 and arr.shape[-1] % cols == 0
        side_in.append(pl.BlockSpec((None,) * len(lead) + (rc, cols),
                                    lambda b, h, lead=lead, cb=colblk: lead + (b * H + h, cb)))
        side_out.append(pl.BlockSpec((rc, cols), lambda b, h: (b * H + h, 0)))
        side_shape.append(jax.ShapeDtypeStruct((rows, cols), BF16))
    outs = pl.pallas_call(
        functools.partial(_attn_kernel, blk=blk, hd=hd, n_side=len(side)),
        grid=(B, H),
        in_specs=[pl.BlockSpec(memory_space=pltpu.SMEM),
                  pl.BlockSpec((4, hd), lambda b, h: (0, 0)),
                  pl.BlockSpec((vhd, 1), lambda b, h: (0, 0)),
                  pl.BlockSpec((None, S, vhd), lambda b, h: (b, 0, h)),
                  pl.BlockSpec((None, S, vhd), lambda b, h: (b, 0, H + h)),
                  pl.BlockSpec((None, S, vhd), lambda b, h: (b, 0, 2 * H + h))] + side_in,
        out_specs=[pl.BlockSpec((None, S, vhd), lambda b, h: (b, 0, h))] + side_out,
        out_shape=[jax.ShapeDtypeStruct((B, S, H * vhd), BF16)] + side_shape,
        scratch_shapes=[pltpu.VMEM((nq, vhd, blk), BF16), pltpu.VMEM((nq, 2, hd, blk), BF16),
                        scores, scores, pltpu.VMEM((2, 1, blk), F32), pltpu.VMEM((2, 1, blk), F32),
                        stat, stat, acc, stat, stat, acc],
        compiler_params=_params("parallel", "parallel"),
        name="diff_attention",
    )(scal, lam_l, subln_l.reshape(vhd, 1), qkv3, qkv3, qkv3, *[s[0] for s in side])
    return (outs[0].reshape(B * S, H * vhd), *outs[1:])


def _sgu_kernel(h_ref, wub_ref, wvb_ref, lng_ref, lnb_ref, ws_ref, bs_ref, o_ref, u_ref,
                *, chunk, groups):
    W = o_ref.shape[1]
    gd = W // groups
    cw = _blk(W, 512)
    r = lax.broadcasted_iota(jnp.int32, (chunk, chunk), 0)
    c = lax.broadcasted_iota(jnp.int32, (chunk, chunk), 1)
    causal = r >= c
    sub = u_ref.shape[1]

    for sb in range(u_ref.shape[0]):
        b0 = sb * sub
        u_sb = u_ref.at[sb]
        h = h_ref[b0:b0 + sub, :]

        def gelu_cols(wb_ref, c0):
            z = _mm(h, wb_ref[:, c0:c0 + cw])
            return 0.5 * z * (1.0 + lax.erf(z * (2.0 ** -0.5)))

        vg = jnp.concatenate([gelu_cols(wvb_ref, c0) for c0 in range(0, W, cw)], axis=1)
        mu = jnp.mean(vg, axis=-1, keepdims=True)
        d = vg - mu
        var = jnp.mean(d * d, axis=-1, keepdims=True)
        vn = ((d * lax.rsqrt(var + EPS)) * lng_ref[...] + lnb_ref[...]).astype(BF16)
        for c0 in range(0, W, cw):
            u_sb[:, c0:c0 + cw] = gelu_cols(wub_ref, c0)
        for g in range(groups):
            w = jnp.where(causal, ws_ref[g], 0.0).astype(BF16)
            bias = bs_ref[:, g:g + 1]
            cols = slice(g * gd, (g + 1) * gd)
            for n in range(sub // chunk):
                rows = slice(n * chunk, (n + 1) * chunk)
                mixed = _mm(w, vn[rows, cols]) + bias
                o_ref[b0 + n * chunk:b0 + (n + 1) * chunk, cols] = (
                    u_sb[rows, cols] * mixed).astype(o_ref.dtype)


def _resident(shape):
    return pl.BlockSpec(shape, lambda *_: (0,) * len(shape), pipeline_mode=pl.Buffered(1))


def _sgu(h, wu, wv, ln_g, ln_b, ws, bs_t, chunk, groups):
    T, D = h.shape
    W = ln_g.shape[0]
    sub = 2 * chunk if T % (2 * chunk) == 0 else chunk
    tm = 4 * sub if T % (4 * sub) == 0 else sub
    return pl.pallas_call(
        functools.partial(_sgu_kernel, chunk=chunk, groups=groups),
        grid=(T // tm,),
        in_specs=[pl.BlockSpec((tm, D), lambda i: (i, 0)),
                  _resident((D, W)), _resident((D, W)),
                  pl.BlockSpec((1, W), lambda i: (0, 0)),
                  pl.BlockSpec((1, W), lambda i: (0, 0)),
                  pl.BlockSpec((groups, chunk, chunk), lambda i: (0, 0, 0)),
                  pl.BlockSpec((chunk, groups), lambda i: (0, 0))],
        out_specs=pl.BlockSpec((tm, W), lambda i: (i, 0)),
        out_shape=jax.ShapeDtypeStruct((T, W), BF16),
        scratch_shapes=[pltpu.VMEM((tm // sub, sub, W), F32)],
        compiler_params=_params("parallel"),
        name="sgu",
    )(h, wu, wv, ln_g.reshape(1, W), ln_b.reshape(1, W), ws, bs_t)


def _merge_kernel(h_ref, o_ref, sg_ref, wga_ref, wgs_ref, wa_ref, ws_ref, ba_ref, bs_ref, out_ref):
    rc = _blk(out_ref.shape[0], 512)
    for r0 in range(0, out_ref.shape[0], rc):
        rows = slice(r0, r0 + rc)
        h = h_ref[rows, :]
        g_attn = jax.nn.sigmoid(_mm(h, wga_ref[...]) + ba_ref[...])
        g_sgu = jax.nn.sigmoid(_mm(h, wgs_ref[...]) + bs_ref[...])
        y = (g_attn * _mm(o_ref[rows, :], wa_ref[...])
             + g_sgu * _mm(sg_ref[rows, :], ws_ref[...]))
        out_ref[rows, :] = y.astype(out_ref.dtype)


def _merge(h, o, sg, wga, wgs, wa, ws, gate_b):
    T, D = h.shape
    tm, tn = _blk(T, 1024), _blk(D, 512)
    nb = D // tn
    row = lambda j, i: (i, 0)
    col = lambda j, i: (0, j)
    return pl.pallas_call(
        _merge_kernel,
        grid=(nb, T // tm),
        in_specs=[pl.BlockSpec((tm, D), row),
                  pl.BlockSpec((tm, o.shape[1]), row),
                  pl.BlockSpec((tm, sg.shape[1]), row),
                  pl.BlockSpec((D, tn), col),
                  pl.BlockSpec((D, tn), col),
                  pl.BlockSpec((o.shape[1], tn), col),
                  pl.BlockSpec((sg.shape[1], tn), col),
                  pl.BlockSpec((1, tn), col),
                  pl.BlockSpec((1, tn), lambda j, i: (0, nb + j))],
        out_specs=pl.BlockSpec((tm, tn), lambda j, i: (i, j)),
        out_shape=jax.ShapeDtypeStruct((T, D), BF16),
        compiler_params=_params("parallel", "parallel"),
        name="merge",
    )(h, o, sg, wga, wgs, wa, ws, gate_b.reshape(1, -1), gate_b.reshape(1, -1))


def _proj_res_norm_kernel(a_ref, w_ref, x_ref, g_ref, *out_refs):
    rc = _blk(x_ref.shape[0], ROW_CHUNK)
    for r0 in range(0, x_ref.shape[0], rc):
        rows = slice(r0, r0 + rc)
        y = x_ref[rows, :] + _mm(a_ref[rows, :], w_ref[...])
        if len(out_refs) == 2:
            out_refs[0][rows, :] = y
        n = y * lax.rsqrt(jnp.mean(y * y, axis=-1, keepdims=True) + EPS)
        out_refs[-1][rows, :] = (n * g_ref[...]).astype(out_refs[-1].dtype)


def _proj_res_norm(a, w, x, g, norm_dtype, emit_x=True):
    T, K = a.shape
    D = w.shape[1]
    tm = _blk(T, 512)
    row = lambda i: (i, 0)
    out_specs = [pl.BlockSpec((tm, D), row)] * (2 if emit_x else 1)
    out_shape = ([jax.ShapeDtypeStruct((T, D), F32)] if emit_x else []) + [
        jax.ShapeDtypeStruct((T, D), norm_dtype)]
    outs = pl.pallas_call(
        _proj_res_norm_kernel,
        grid=(T // tm,),
        in_specs=[pl.BlockSpec((tm, K), row),
                  _resident((K, D)),
                  pl.BlockSpec((tm, D), row),
                  pl.BlockSpec((1, D), lambda i: (0, 0))],
        out_specs=out_specs,
        out_shape=out_shape,
        compiler_params=_params("parallel"),
        name="proj_res_norm",
    )(a, w, x, g.reshape(1, D))
    return outs if emit_x else outs[0]


def _ffn_in_kernel(h_ref, wa_ref, wb_ref, o_ref, wab_ref, wbb_ref):
    _cache_bf16(wa_ref, wab_ref)
    _cache_bf16(wb_ref, wbb_ref)
    rc = _blk(o_ref.shape[0], DOT_ROWS)
    for r0 in range(0, o_ref.shape[0], rc):
        rows = slice(r0, r0 + rc)
        h = h_ref[rows, :]
        a = _mm(h, wab_ref[...])
        b = _mm(h, wbb_ref[...])
        o_ref[rows, :] = (a * jax.nn.sigmoid(a) * b).astype(o_ref.dtype)


def _ffn_in(h, w_ffn_in, layer):
    T, D = h.shape
    F = w_ffn_in.shape[2] // 2
    tm, tn = _blk(T, 2 * DOT_ROWS), _blk(F, 512)
    nb = F // tn
    return pl.pallas_call(
        _ffn_in_kernel,
        grid=(nb, T // tm),
        in_specs=[pl.BlockSpec((tm, D), lambda j, i: (i, 0)),
                  pl.BlockSpec((None, D, tn), lambda j, i: (layer, 0, j)),
                  pl.BlockSpec((None, D, tn), lambda j, i: (layer, 0, nb + j))],
        out_specs=pl.BlockSpec((tm, tn), lambda j, i: (i, j)),
        out_shape=jax.ShapeDtypeStruct((T, F), BF16),
        scratch_shapes=[pltpu.VMEM((D, tn), BF16)] * 2,
        compiler_params=_params("parallel", "arbitrary"),
        name="ffn_in",
    )(h, w_ffn_in, w_ffn_in)


def kernel(x, positions, norm_attn, w_in, gate_b, lam, subln, sgu_ln_g, sgu_ln_b, w_spatial,
           b_spatial, w_branch, w_out, norm_ffn, w_ffn_in, w_ffn_out, norm_final):
    B, S, D = x.shape
    T = B * S
    depth = w_in.shape[0]
    hd = lam.shape[-1]
    H = D // subln.shape[-1]
    groups, chunk = w_spatial.shape[1], w_spatial.shape[2]
    W = sgu_ln_g.shape[-1]
    assert S % chunk == 0 and w_in.shape[2] == 3 * D + 2 * W + 2 * D

    inv_freq = ROPE_THETA ** (-jnp.arange(0, hd, 2, dtype=F32) / hd)
    invf = jnp.concatenate([inv_freq, inv_freq]).reshape(1, hd)
    cos, sin = _rope_tables(positions.reshape(T, 1), invf)

    xf = x.reshape(T, D)
    h = _rmsnorm(xf, norm_attn[0], BF16)
    for l in range(depth):
        lam_init = 0.8 - 0.6 * math.exp(-0.3 * l)
        qkv = _in_proj(h, w_in, l, 0, 3 * D, cos, sin, n_rope_cols=2 * D)
        side = [(w_in, (l,), (3 * D) // W, W), (w_in, (l,), (3 * D) // W + 1, W),
                (w_in, (l,), (3 * D + 2 * W) // D, D), (w_in, (l,), (3 * D + 2 * W) // D + 1, D),
                (w_branch, (l, 0), 0, D), (w_branch, (l, 1), 0, D),
                (w_out, (l,), 0, D), (w_ffn_out, (l,), 0, D)]
        o, w_zu, w_zv, w_ga, w_gs, w_ba, w_bs, w_o, w_f = _diff_attention(
            qkv, lam[l], subln[l], lam_init, B, S, H, hd, side)
        sg = _sgu(h, w_zu, w_zv, sgu_ln_g[l], sgu_ln_b[l], w_spatial[l], b_spatial[l].T,
                  chunk, groups)
        merged = _merge(h, o, sg, w_ga, w_gs, w_ba, w_bs, gate_b[l])
        xf, h2 = _proj_res_norm(merged, w_o, xf, norm_ffn[l], BF16)

        act = _ffn_in(h2, w_ffn_in, l)
        if l + 1 < depth:
            xf, h = _proj_res_norm(act, w_f, xf, norm_attn[l + 1], BF16)
        else:
            out = _proj_res_norm(act, w_f, xf, norm_final, x.dtype, emit_x=False)
    return out.reshape(B, S, D)
```
